```python
import math
import jax, jax.numpy as jnp
from jax import lax
import numpy as np

D_MODEL = 4096
BATCH = 1
SEQ = 8192
DEPTH = 2
DEC_BATCH = 8
DEC_SEQ = 16
PAST_LEN = 2048

CHUNK = 64
N_A_LAYERS = DEPTH // 2
N_B_LAYERS = DEPTH - N_A_LAYERS
SSM_EXPAND = 2
D_INNER = SSM_EXPAND * D_MODEL
SSM_HEADDIM = 64
SSM_HEADS = D_INNER // SSM_HEADDIM
SSM_GROUPS = 8
D_STATE = 128
CONV_W = 4
CONV_DIM = D_INNER + 2 * SSM_GROUPS * D_STATE
IN_PROJ_DIM = D_INNER + CONV_DIM + SSM_HEADS
SSD_CHUNK = CHUNK
FOX_HEAD_DIM = 128
FOX_HEADS = D_MODEL // FOX_HEAD_DIM
Q_BLOCK = 128
FGATE_BIAS_LO = 1.0
FGATE_BIAS_HI = 6.0
D_FF = 256 * ((8 * D_MODEL // 3 + 255) // 256)
FFN_CONV_W = 3
EPS = 1e-6

kernel_name = 'yoco_mamba2_fox_convffn_stream_step'


def rmsnorm(x):
    xf = x.astype(jnp.float32)
    return (xf * lax.rsqrt(jnp.mean(xf * xf, axis=-1, keepdims=True) + EPS)).astype(x.dtype)


def ada_mod(c, w, b):
    mod = jax.nn.silu(c) @ w + b
    shift, scale, gate = jnp.split(mod, 3, axis=-1)
    return shift[:, None], scale[:, None], gate[:, None]


def causal_dwconv(x, buf, w, b):
    xp = jnp.concatenate([buf.astype(x.dtype), x], axis=1)
    y = lax.conv_general_dilated(xp, w[:, None, :].astype(x.dtype), window_strides=(1,),
                                 padding='VALID', dimension_numbers=('NWC', 'WIO', 'NWC'),
                                 feature_group_count=x.shape[-1])
    return y + b, xp[:, xp.shape[1] - (w.shape[0] - 1):]


def ssd_scan(x, dt, A, Bm, Cm, h0, chunk):
    b, T, H, P = x.shape
    G, N = Bm.shape[2], Bm.shape[3]
    hpg = H // G
    nc = T // chunk
    xs = (x.astype(jnp.float32) * dt[..., None]).reshape(b, nc, chunk, G, hpg, P)
    dA = (dt * A).reshape(b, nc, chunk, G, hpg)
    Bc = Bm.astype(jnp.float32).reshape(b, nc, chunk, G, N)
    Cc = Cm.astype(jnp.float32).reshape(b, nc, chunk, G, N)
    a_cum = jnp.cumsum(dA, axis=2)
    seg = a_cum[:, :, :, None] - a_cum[:, :, None]
    tril = jnp.tril(jnp.ones((chunk, chunk), dtype=bool))
    decay = jnp.exp(jnp.where(tril[:, :, None, None], seg, -jnp.inf))
    cb = jnp.einsum('bclgn,bcsgn->bclsg', Cc, Bc)
    y_diag = jnp.einsum('bclsgh,bcsghp->bclghp', cb[..., None] * decay, xs)
    decay_s = jnp.exp(a_cum[:, :, -1:] - a_cum)
    states = jnp.einsum('bclgn,bclghp->bcghpn', Bc, xs * decay_s[..., None])
    chunk_decay = jnp.exp(a_cum[:, :, -1])

    def step(h, inp):
        st, dcy = inp
        return h * dcy[..., None, None] + st, h

    h0g = h0.astype(jnp.float32).reshape(b, G, hpg, P, N)
    h_fin, h_in = lax.scan(step, h0g, (jnp.moveaxis(states, 1, 0), jnp.moveaxis(chunk_decay, 1, 0)))
    h_in = jnp.moveaxis(h_in, 0, 1)
    y_off = jnp.einsum('bclgn,bcghpn->bclghp', Cc, h_in) * jnp.exp(a_cum)[..., None]
    y = (y_diag + y_off).reshape(b, T, H, P)
    return y, h_fin.reshape(b, H, P, N)


def mamba2_mixer(h, conv_buf, ssm0, w_in, conv_w, conv_b, dt_bias, a_log, d_skip, norm_w, w_out, chunk):
    b, T, _ = h.shape
    zxbcdt = h @ w_in
    z, xbc, dt = jnp.split(zxbcdt, [D_INNER, D_INNER + CONV_DIM], axis=-1)
    xbc, conv_new = causal_dwconv(xbc, conv_buf, conv_w, conv_b)
    xbc = jax.nn.silu(xbc)
    xs, Bm, Cm = jnp.split(xbc, [D_INNER, D_INNER + SSM_GROUPS * D_STATE], axis=-1)
    xs = xs.reshape(b, T, SSM_HEADS, SSM_HEADDIM)
    Bm = Bm.reshape(b, T, SSM_GROUPS, D_STATE)
    Cm = Cm.reshape(b, T, SSM_GROUPS, D_STATE)
    dt = jax.nn.softplus(dt.astype(jnp.float32) + dt_bias.astype(jnp.float32))
    A = -jnp.exp(a_log.astype(jnp.float32))
    y, ssm_new = ssd_scan(xs, dt, A, Bm, Cm, ssm0, chunk)
    y = y + d_skip.astype(jnp.float32)[:, None] * xs.astype(jnp.float32)
    y = y.reshape(b, T, D_INNER) * jax.nn.silu(z.astype(jnp.float32))
    y = rmsnorm(y.reshape(b, T, SSM_GROUPS, D_INNER // SSM_GROUPS)).reshape(b, T, D_INNER)
    y = (y * norm_w.astype(jnp.float32)).astype(h.dtype)
    return y @ w_out, conv_new, ssm_new


def conv_ffn(h, buf, w_up, conv_w, conv_b, w_down):
    a, g = jnp.split(h @ w_up, 2, axis=-1)
    g, buf_new = causal_dwconv(g, buf, conv_w, conv_b)
    return (jax.nn.silu(g) * a) @ w_down, buf_new


def fox_attention(q, k, v, c_q, c_k, q_pos, k_pos, block):
    b, Tq, H, Dh = q.shape
    nb = Tq // block
    scale = Dh ** -0.5
    qb = q.reshape(b, nb, block, H, Dh).swapaxes(0, 1)
    cqb = c_q.reshape(b, nb, block, H).swapaxes(0, 1)
    pqb = q_pos.reshape(nb, block)
    ckT = c_k.transpose(0, 2, 1)

    def one_block(args):
        qi, cqi, pqi = args
        s = jnp.einsum('bqhd,bkhd->bhqk', qi, k, preferred_element_type=jnp.float32) * scale
        s = s + cqi.transpose(0, 2, 1)[..., None] - ckT[:, :, None, :]
        s = jnp.where(k_pos[None, None, None, :] <= pqi[None, None, :, None], s, -jnp.inf)
        pr = jax.nn.softmax(s, axis=-1)
        return jnp.einsum('bhqk,bkhd->bqhd', pr.astype(v.dtype), v)

    out = lax.map(one_block, (qb, cqb, pqb))
    return out.swapaxes(0, 1).reshape(b, Tq, H, Dh)


def run_trunk(x, c, ssm0, conv0, ffn0, k_past, v_past, lf_past, p):
    b, T, _ = x.shape
    past = k_past.shape[1]
    ssd_chunk = min(SSD_CHUNK, T)
    q_block = min(Q_BLOCK, T)
    new_ssm, new_conv, new_ffn = [], [], []
    for l in range(DEPTH):
        shift, scale, gate = ada_mod(c, p['w_ada'][l, 0], p['b_ada'][l, 0])
        h = rmsnorm(x) * (1 + scale) + shift
        if l < N_A_LAYERS:
            y, cbuf, sst = mamba2_mixer(h, conv0[l], ssm0[l], p['m_w_in'][l], p['m_conv_w'][l],
                                        p['m_conv_b'][l], p['m_dt_bias'][l], p['m_a_log'][l],
                                        p['m_d'][l], p['m_norm_w'][l], p['m_w_out'][l], ssd_chunk)
            new_conv.append(cbuf)
            new_ssm.append(sst)
        else:
            j = l - N_A_LAYERS
            q = (h @ p['w_q'][j]).reshape(b, T, FOX_HEADS, FOX_HEAD_DIM)
            o = fox_attention(q, k_all, v_all, c_q, c_all, q_pos, k_pos, q_block)
            y = o.reshape(b, T, D_MODEL) @ p['w_o'][j]
        x = x + gate * y
        shift, scale, gate = ada_mod(c, p['w_ada'][l, 1], p['b_ada'][l, 1])
        h = rmsnorm(x) * (1 + scale) + shift
        y, fbuf = conv_ffn(h, ffn0[l], p['f_w_up'][l], p['f_conv_w'][l], p['f_conv_b'][l], p['f_w_down'][l])
        new_ffn.append(fbuf)
        x = x + gate * y
        if l == N_A_LAYERS - 1:
            hkv = rmsnorm(x) * p['kv_norm_w']
            k_new, v_new = jnp.split(hkv @ p['w_kv'], 2, axis=-1)
            k_new = k_new.reshape(b, T, FOX_HEADS, FOX_HEAD_DIM)
            v_new = v_new.reshape(b, T, FOX_HEADS, FOX_HEAD_DIM)
            lf_new = jax.nn.log_sigmoid((hkv @ p['w_fgate']).astype(jnp.float32)
                                        + p['b_fgate'].astype(jnp.float32))
            k_all = jnp.concatenate([k_past.astype(x.dtype), k_new], axis=1)
            v_all = jnp.concatenate([v_past.astype(x.dtype), v_new], axis=1)
            c_all = jnp.cumsum(jnp.concatenate([lf_past.astype(jnp.float32), lf_new], axis=1), axis=1)
            c_q = c_all[:, past:]
            q_pos = past + jnp.arange(T)
            k_pos = jnp.arange(past + T)
    y_out = rmsnorm(x) * p['final_norm_w']
    return (y_out, jnp.stack(new_ssm).astype(x.dtype), jnp.stack(new_conv), jnp.stack(new_ffn),
            k_new, v_new, lf_new.astype(x.dtype))


def setup_inputs(seed: int = 0) -> dict:
    key = jax.random.key(seed)
    ks = iter(jax.random.split(key, 40))

    def nrm(shape, s):
        return jax.random.normal(next(ks), shape, jnp.float32) * s

    fg_bias = jnp.linspace(FGATE_BIAS_LO, FGATE_BIAS_HI, FOX_HEADS)
    u = jax.random.uniform(next(ks), (N_A_LAYERS, SSM_HEADS), minval=math.log(1e-3), maxval=math.log(1e-1))
    dt0 = jnp.exp(u)
    return {
        'x_prompt': nrm((BATCH, SEQ, D_MODEL), 1.0),
        'x_sample': nrm((DEC_BATCH, DEC_SEQ, D_MODEL), 1.0),
        'c_prompt': nrm((BATCH, D_MODEL), 1.0),
        'c_sample': nrm((DEC_BATCH, D_MODEL), 1.0),
        'cache_k': nrm((DEC_BATCH, PAST_LEN, FOX_HEADS, FOX_HEAD_DIM), 1.0),
        'cache_v': nrm((DEC_BATCH, PAST_LEN, FOX_HEADS, FOX_HEAD_DIM), 1.0),
        'cache_logf': jax.nn.log_sigmoid(fg_bias + nrm((DEC_BATCH, PAST_LEN, FOX_HEADS), 1.0)),
        'state_ssm': nrm((N_A_LAYERS, DEC_BATCH, SSM_HEADS, SSM_HEADDIM, D_STATE), 0.1),
        'state_conv': nrm((N_A_LAYERS, DEC_BATCH, CONV_W - 1, CONV_DIM), 1.0),
        'state_ffn_conv': nrm((DEPTH, DEC_BATCH, FFN_CONV_W - 1, D_FF), 1.0),
        'w_ada': nrm((DEPTH, 2, D_MODEL, 3 * D_MODEL), 0.5 * D_MODEL ** -0.5),
        'b_ada': nrm((DEPTH, 2, 3 * D_MODEL), 0.01),
        'm_w_in': nrm((N_A_LAYERS, D_MODEL, IN_PROJ_DIM), D_MODEL ** -0.5),
        'm_conv_w': nrm((N_A_LAYERS, CONV_W, CONV_DIM), 0.5),
        'm_conv_b': nrm((N_A_LAYERS, CONV_DIM), 0.01),
        'm_dt_bias': dt0 + jnp.log(-jnp.expm1(-dt0)),
        'm_a_log': jnp.log(jax.random.uniform(next(ks), (N_A_LAYERS, SSM_HEADS), minval=1.0, maxval=16.0)),
        'm_d': 1.0 + nrm((N_A_LAYERS, SSM_HEADS), 0.1),
        'm_norm_w': 1.0 + nrm((N_A_LAYERS, D_INNER), 0.1),
        'm_w_out': nrm((N_A_LAYERS, D_INNER, D_MODEL), D_INNER ** -0.5),
        'kv_norm_w': 1.0 + nrm((D_MODEL,), 0.1),
        'w_kv': nrm((D_MODEL, 2 * D_MODEL), D_MODEL ** -0.5),
        'w_fgate': nrm((D_MODEL, FOX_HEADS), D_MODEL ** -0.5),
        'b_fgate': fg_bias + nrm((FOX_HEADS,), 0.1),
        'w_q': nrm((N_B_LAYERS, D_MODEL, D_MODEL), D_MODEL ** -0.5),
        'w_o': nrm((N_B_LAYERS, D_MODEL, D_MODEL), D_MODEL ** -0.5),
        'f_w_up': nrm((DEPTH, D_MODEL, 2 * D_FF), D_MODEL ** -0.5),
        'f_conv_w': nrm((DEPTH, FFN_CONV_W, D_FF), 0.5),
        'f_conv_b': nrm((DEPTH, D_FF), 0.01),
        'f_w_down': nrm((DEPTH, D_FF, D_MODEL), D_FF ** -0.5),
        'final_norm_w': 1.0 + nrm((D_MODEL,), 0.1),
    }


def reference(x_prompt, x_sample, c_prompt, c_sample, cache_k, cache_v, cache_logf,
              state_ssm, state_conv, state_ffn_conv, w_ada, b_ada, m_w_in, m_conv_w,
              m_conv_b, m_dt_bias, m_a_log, m_d, m_norm_w, m_w_out, kv_norm_w, w_kv,
              w_fgate, b_fgate, w_q, w_o, f_w_up, f_conv_w, f_conv_b, f_w_down, final_norm_w):
    params = dict(w_ada=w_ada, b_ada=b_ada, m_w_in=m_w_in, m_conv_w=m_conv_w, m_conv_b=m_conv_b,
                  m_dt_bias=m_dt_bias, m_a_log=m_a_log, m_d=m_d, m_norm_w=m_norm_w, m_w_out=m_w_out,
                  kv_norm_w=kv_norm_w, w_kv=w_kv, w_fgate=w_fgate, b_fgate=b_fgate, w_q=w_q, w_o=w_o,
                  f_w_up=f_w_up, f_conv_w=f_conv_w, f_conv_b=f_conv_b, f_w_down=f_w_down,
                  final_norm_w=final_norm_w)
    bp, dtp = x_prompt.shape[0], x_prompt.dtype
    ssm0 = jnp.zeros((N_A_LAYERS, bp, SSM_HEADS, SSM_HEADDIM, D_STATE), jnp.float32)
    conv0 = jnp.zeros((N_A_LAYERS, bp, CONV_W - 1, CONV_DIM), dtp)
    ffn0 = jnp.zeros((DEPTH, bp, FFN_CONV_W - 1, D_FF), dtp)
    kv0 = jnp.zeros((bp, 0, FOX_HEADS, FOX_HEAD_DIM), dtp)
    lf0 = jnp.zeros((bp, 0, FOX_HEADS), jnp.float32)
    y_prompt, p_ssm, p_conv, p_ffn, p_k, p_v, p_logf = run_trunk(
        x_prompt, c_prompt, ssm0, conv0, ffn0, kv0, kv0, lf0, params)
    y_sample, s_ssm, s_conv, s_ffn, s_k, s_v, s_logf = run_trunk(
        x_sample, c_sample, state_ssm, state_conv, state_ffn_conv, cache_k, cache_v, cache_logf, params)
    return (y_prompt, y_sample, p_ssm, p_conv, p_ffn, p_k, p_v, p_logf,
            s_ssm, s_conv, s_ffn, s_k, s_v, s_logf)
```

```python
import functools
import math

import jax
import jax.numpy as jnp
from jax import lax
from jax.experimental import pallas as pl
from jax.experimental.pallas import tpu as pltpu

F32 = jnp.float32
MXU_DTYPE = jnp.bfloat16
EPS = 1e-6
LANES = 128
SUBLANES = 8
VMEM_LIMIT_BYTES = 56 * 1024 * 1024
SSD_CHUNK = 64
NEG_INF = float("-inf")


def _tile(dim, pref, align):
    t = (min(pref, dim) // align) * align
    while t >= align:
        if dim % t == 0:
            return t
        t -= align
    return dim


def _params(*sem):
    return pltpu.CompilerParams(dimension_semantics=sem, vmem_limit_bytes=VMEM_LIMIT_BYTES)


def _silu(x):
    return x * jax.nn.sigmoid(x)


def _softplus(x):
    return jnp.maximum(x, 0.0) + jnp.log1p(jnp.exp(-jnp.abs(x)))


def _split3(a):
    a1 = a.astype(MXU_DTYPE)
    r1 = a - a1.astype(F32)
    a2 = r1.astype(MXU_DTYPE)
    r2 = r1 - a2.astype(F32)
    return a1, a2, r2.astype(MXU_DTYPE)


def _dot01(m01, a, dims=(((1,), (0,)), ((), ()))):
    m = m01.astype(MXU_DTYPE)
    out = None
    for t in _split3(a):
        part = lax.dot_general(m, t, dims, preferred_element_type=F32)
        out = part if out is None else out + part
    return out


def _ada_kernel(c_ref, w_ref, b_ref, o_ref):
    a = _silu(c_ref[...]).astype(MXU_DTYPE)
    w = w_ref[...].astype(MXU_DTYPE)
    o_ref[...] = jnp.dot(a, w, preferred_element_type=F32) + b_ref[...]


def _ada_all(c_rows, w_ada, b_ada):
    r, d = c_rows.shape
    s = w_ada.shape[0] * w_ada.shape[1]
    n = w_ada.shape[-1]
    w = w_ada.reshape(s, d, n)
    b = b_ada.reshape(s, 1, n)
    tn = _tile(n, 512, LANES)
    return pl.pallas_call(
        _ada_kernel,
        out_shape=jax.ShapeDtypeStruct((s, r, n), F32),
        grid=(s, n // tn),
        in_specs=[
            pl.BlockSpec((r, d), lambda i, j: (0, 0)),
            pl.BlockSpec((None, d, tn), lambda i, j: (i, 0, j)),
            pl.BlockSpec((None, 1, tn), lambda i, j: (i, 0, j)),
        ],
        out_specs=pl.BlockSpec((None, r, tn), lambda i, j: (i, 0, j)),
        compiler_params=_params("parallel", "parallel"),
        name="ada_mod",
    )(c_rows, w, b)


def _norm_kernel(*refs, plus_one, has_add):
    x_ref, mul_ref = refs[0], refs[1]
    add_ref = refs[2] if has_add else None
    o_ref = refs[-1]
    x = x_ref[...]
    y = x * lax.rsqrt(jnp.mean(x * x, axis=-1, keepdims=True) + EPS)
    m = mul_ref[...]
    if plus_one:
        m = 1.0 + m
    y = y * m
    if has_add:
        y = y + add_ref[...]
    o_ref[...] = y.astype(o_ref.dtype)


def _norm(x, mul, add=None, *, plus_one, out_dtype):
    b, t, d = x.shape
    tt = _tile(t, 256, 16)
    per_batch = mul.shape[0] == b and b > 1
    mod_map = (lambda i, j: (i, 0, 0)) if per_batch else (lambda i, j: (0, 0, 0))
    ins = [x, mul]
    specs = [pl.BlockSpec((None, tt, d), lambda i, j: (i, j, 0)),
             pl.BlockSpec((None, 1, d), mod_map)]
    if add is not None:
        ins.append(add)
        specs.append(pl.BlockSpec((None, 1, d), mod_map))
    return pl.pallas_call(
        functools.partial(_norm_kernel, plus_one=plus_one, has_add=add is not None),
        out_shape=jax.ShapeDtypeStruct((b, t, d), out_dtype),
        grid=(b, t // tt),
        in_specs=specs,
        out_specs=pl.BlockSpec((None, tt, d), lambda i, j: (i, j, 0)),
        compiler_params=_params("parallel", "parallel"),
        name="rmsnorm_mod",
    )(*ins)


def _mm_kernel(*refs, nk, epilogue):
    x_ref, w_ref = refs[0], refs[1]
    pos = 2
    if epilogue == "resid":
        res_ref, gate_ref = refs[2], refs[3]
        pos = 4
    elif epilogue == "logsig":
        bias_ref = refs[2]
        pos = 3
    o_ref = refs[pos]
    part = jnp.dot(x_ref[...], w_ref[...].astype(MXU_DTYPE), preferred_element_type=F32)

    def finish(acc):
        if epilogue == "resid":
            acc = res_ref[...] + gate_ref[...] * acc
        elif epilogue == "logsig":
            acc = -_softplus(-(acc + bias_ref[...]))
        o_ref[...] = acc.astype(o_ref.dtype)

    if nk == 1:
        finish(part)
    else:
        acc_ref = refs[pos + 1]
        k = pl.program_id(2)

        @pl.when(k == 0)
        def _():
            acc_ref[...] = part

        @pl.when(k > 0)
        def _():
            acc_ref[...] += part

        @pl.when(k == nk - 1)
        def _():
            finish(acc_ref[...])


def _matmul(x, w, *, n_off=0, n=None, out_dtype=F32, bm=1024, bn=512, bk=4096,
            epilogue=None, res=None, gate=None, bias=None):
    m, kdim = x.shape
    n = w.shape[1] if n is None else n
    bm = _tile(m, bm, 16)
    bn = _tile(math.gcd(n, n_off), bn, LANES)
    bk = _tile(kdim, bk, LANES)
    assert n_off % bn == 0
    joff = n_off // bn
    nk = kdim // bk
    ins = [x, w]
    x_mode = dict(pipeline_mode=pl.Buffered(1)) if nk == 1 and n // bn > 1 else {}
    specs = [pl.BlockSpec((bm, bk), lambda i, j, k: (i, k), **x_mode),
             pl.BlockSpec((bk, bn), lambda i, j, k: (k, j + joff))]
    if epilogue == "resid":
        ins += [res, gate]
        specs.append(pl.BlockSpec((bm, bn), lambda i, j, k: (i, j)))
        if gate.shape[0] == 1:
            specs.append(pl.BlockSpec((1, bn), lambda i, j, k: (0, j)))
        else:
            specs.append(pl.BlockSpec((bm, bn), lambda i, j, k: (i, j)))
    elif epilogue == "logsig":
        ins.append(bias)
        specs.append(pl.BlockSpec((1, bn), lambda i, j, k: (0, j)))
    scratch = [pltpu.VMEM((bm, bn), F32)] if nk > 1 else []
    return pl.pallas_call(
        functools.partial(_mm_kernel, nk=nk, epilogue=epilogue),
        out_shape=jax.ShapeDtypeStruct((m, n), out_dtype),
        grid=(m // bm, n // bn, nk),
        in_specs=specs,
        out_specs=pl.BlockSpec((bm, bn), lambda i, j, k: (i, j)),
        scratch_shapes=scratch,
        compiler_params=_params("parallel", "parallel", "arbitrary"),
        name="matmul_" + (epilogue or "plain"),
    )(*ins)


def _conv_kernel(*refs, width, tt, gated):
    if gated:
        a_ref, x_ref, buf_ref, w_ref, b_ref, y_ref, tail_ref, xx_ref = refs
    else:
        x_ref, buf_ref, w_ref, b_ref, y_ref, tail_ref, xx_ref = refs
    ti = pl.program_id(2)

    @pl.when(ti == 0)
    def _():
        xx_ref[0:SUBLANES, :] = buf_ref[...]

    xx_ref[SUBLANES:SUBLANES + tt, :] = x_ref[...]
    acc = b_ref[...]
    for k in range(width):
        lo = SUBLANES - (width - 1) + k
        acc = acc + w_ref[k:k + 1, :] * xx_ref[lo:lo + tt, :]
    y = _silu(acc)
    if gated:
        y = y * a_ref[...]
    y_ref[...] = y.astype(y_ref.dtype)
    tail = xx_ref[tt:tt + SUBLANES, :]
    xx_ref[0:SUBLANES, :] = tail
    tail_ref[...] = tail


def _causal_conv(src, buf, w, b, *, col_off, cdim, tc, tt, out_dtype, gate_col_off=None):
    bsz, t, _ = src.shape
    width = w.shape[0]
    assert t >= SUBLANES and width - 1 <= SUBLANES
    tt = _tile(t, tt, SUBLANES)
    tc = _tile(math.gcd(cdim, col_off, gate_col_off or 0), tc, LANES)
    assert col_off % tc == 0
    coff = col_off // tc
    buf8 = jnp.pad(buf.astype(F32), ((0, 0), (SUBLANES - (width - 1), 0), (0, 0)))
    w8 = jnp.pad(w, ((0, SUBLANES - width), (0, 0)))
    b2 = b.reshape(1, cdim)
    gated = gate_col_off is not None
    ins, specs = [], []
    if gated:
        assert gate_col_off % tc == 0
        goff = gate_col_off // tc
        ins.append(src)
        specs.append(pl.BlockSpec((None, tt, tc), lambda i, j, k: (i, k, j + goff)))
    ins += [src, buf8, w8, b2]
    specs += [
        pl.BlockSpec((None, tt, tc), lambda i, j, k: (i, k, j + coff)),
        pl.BlockSpec((None, SUBLANES, tc), lambda i, j, k: (i, 0, j)),
        pl.BlockSpec((SUBLANES, tc), lambda i, j, k: (0, j)),
        pl.BlockSpec((1, tc), lambda i, j, k: (0, j)),
    ]
    return pl.pallas_call(
        functools.partial(_conv_kernel, width=width, tt=tt, gated=gated),
        out_shape=(jax.ShapeDtypeStruct((bsz, t, cdim), out_dtype),
                   jax.ShapeDtypeStruct((bsz, SUBLANES, cdim), F32)),
        grid=(bsz, cdim // tc, t // tt),
        in_specs=specs,
        out_specs=(pl.BlockSpec((None, tt, tc), lambda i, j, k: (i, k, j)),
                   pl.BlockSpec((None, SUBLANES, tc), lambda i, j, k: (i, 0, j))),
        scratch_shapes=[pltpu.VMEM((tt + SUBLANES, tc), F32)],
        compiler_params=_params("parallel", "parallel", "arbitrary"),
        name="causal_conv_gated" if gated else "causal_conv",
    )(*ins)


def _ssd_kernel(x_ref, b_ref, c_ref, z_ref, dt_ref, dtb_ref, alog_ref, dsk_ref, nw_ref, h0_ref,
                y_ref, hout_ref, ht_ref, yg_ref, *, hpg, hd, ln, nheads):
    g = pl.program_id(1)
    c = pl.program_id(2)
    gw = hpg * hd
    per_blk = LANES // hd
    nblk = gw // LANES

    @pl.when(c == 0)
    def _():
        ht_ref[...] = h0_ref[...].T

    dt = _softplus(dt_ref[...] + dtb_ref[...])
    a_neg = -jnp.exp(alog_ref[...])
    shift = lax.rem(nheads - g * hpg, nheads)
    dt_g = pltpu.roll(dt, shift, axis=1)
    da_g = pltpu.roll(dt * a_neg, shift, axis=1)
    dsk_g = pltpu.roll(jnp.broadcast_to(dsk_ref[...], (SUBLANES, nheads)), shift, axis=1)

    row = lax.broadcasted_iota(jnp.int32, (ln, ln), 0)
    col = lax.broadcasted_iota(jnp.int32, (ln, ln), 1)
    tril = row >= col
    a_cum = _dot01(tril, da_g)
    hp = max(hpg, SUBLANES)
    sel = (lax.broadcasted_iota(jnp.int32, (hp, nheads), 0)
           == lax.broadcasted_iota(jnp.int32, (hp, nheads), 1))
    nt = (((1,), (1,)), ((), ()))
    a_cum_t = _dot01(sel, a_cum, nt)

    bmat = b_ref[...].astype(MXU_DTYPE)
    cmat = c_ref[...].astype(MXU_DTYPE)
    cb = lax.dot_general(cmat, bmat, nt, preferred_element_type=F32)

    lane = lax.broadcasted_iota(jnp.int32, (1, LANES), 1)
    ssq = jnp.zeros((ln, 1), F32)
    for j in range(nblk):
        lo, hi = j * LANES, (j + 1) * LANES
        x_blk = x_ref[:, lo:hi]
        acol = dcol = dsk_row = None
        for w in range(per_blk):
            i = j * per_blk + w
            a_b = jnp.broadcast_to(a_cum[:, i:i + 1], (ln, LANES))
            d_b = jnp.broadcast_to(dt_g[:, i:i + 1], (ln, LANES))
            s_b = jnp.broadcast_to(dsk_g[0:1, i:i + 1], (1, LANES))
            if w == 0:
                acol, dcol, dsk_row = a_b, d_b, s_b
            else:
                in_w = lane >= w * hd
                acol = jnp.where(in_w, a_b, acol)
                dcol = jnp.where(in_w, d_b, dcol)
                dsk_row = jnp.where(in_w, s_b, dsk_row)
        alast = acol[ln - 1:ln, :]
        xdt = x_blk * dcol
        xdt_m = xdt.astype(MXU_DTYPE)
        xw_m = (xdt * jnp.exp(alast - acol)).astype(MXU_DTYPE)
        ydiag = None
        for w in range(per_blk):
            i = j * per_blk + w
            seg = acol[:, w * hd:w * hd + 1] - a_cum_t[i:i + 1, :]
            dec = jnp.exp(jnp.where(tril, seg, NEG_INF))
            yw = jnp.dot((cb * dec).astype(MXU_DTYPE), xdt_m, preferred_element_type=F32)
            ydiag = yw if w == 0 else jnp.where(lane >= w * hd, yw, ydiag)
        h_blk = ht_ref[:, lo:hi]
        yoff = jnp.dot(cmat, h_blk.astype(MXU_DTYPE), preferred_element_type=F32) * jnp.exp(acol)
        st = lax.dot_general(bmat, xw_m, (((0,), (0,)), ((), ())), preferred_element_type=F32)
        ht_ref[:, lo:hi] = h_blk * jnp.exp(alast) + st
        y = ydiag + yoff + dsk_row * x_blk
        y = y * _silu(z_ref[:, lo:hi])
        yg_ref[:, lo:hi] = y
        ssq = ssq + jnp.sum(y * y, axis=-1, keepdims=True)

    inv = lax.rsqrt(ssq * (1.0 / gw) + EPS)
    y_ref[...] = (yg_ref[...] * inv * nw_ref[...]).astype(y_ref.dtype)

    @pl.when(c == pl.num_programs(2) - 1)
    def _():
        hout_ref[...] = ht_ref[...].T


def _ssd(xbc, zx, dt_raw, dt_bias, a_log, d_skip, norm_w, h0, *, d_inner, ngroups, nstate, chunk):
    bsz, t, _ = xbc.shape
    nheads = dt_raw.shape[-1]
    hd = d_inner // nheads
    hpg = nheads // ngroups
    gw = hpg * hd
    assert nheads <= LANES
    assert LANES % hd == 0 and gw % LANES == 0 and nstate % LANES == 0 and t % chunk == 0
    nc = t // chunk
    h0g = h0.astype(F32).reshape(bsz, ngroups, gw, nstate)
    vec = lambda a: a.astype(F32).reshape(1, nheads)
    xoff = d_inner // nstate
    kern = functools.partial(_ssd_kernel, hpg=hpg, hd=hd, ln=chunk, nheads=nheads)
    y, hout = pl.pallas_call(
        kern,
        out_shape=(jax.ShapeDtypeStruct((bsz, t, d_inner), MXU_DTYPE),
                   jax.ShapeDtypeStruct((bsz, ngroups, gw, nstate), F32)),
        grid=(bsz, ngroups, nc),
        in_specs=[
            pl.BlockSpec((None, chunk, gw), lambda b, g, c: (b, c, g)),
            pl.BlockSpec((None, chunk, nstate), lambda b, g, c: (b, c, xoff + g)),
            pl.BlockSpec((None, chunk, nstate), lambda b, g, c: (b, c, xoff + ngroups + g)),
            pl.BlockSpec((None, chunk, gw), lambda b, g, c: (b, c, g)),
            pl.BlockSpec((None, chunk, nheads), lambda b, g, c: (b, c, 0)),
            pl.BlockSpec((1, nheads), lambda b, g, c: (0, 0)),
            pl.BlockSpec((1, nheads), lambda b, g, c: (0, 0)),
            pl.BlockSpec((1, nheads), lambda b, g, c: (0, 0)),
            pl.BlockSpec((1, gw), lambda b, g, c: (0, g)),
            pl.BlockSpec((None, None, gw, nstate), lambda b, g, c: (b, g, 0, 0)),
        ],
        out_specs=(pl.BlockSpec((None, chunk, gw), lambda b, g, c: (b, c, g)),
                   pl.BlockSpec((None, None, gw, nstate), lambda b, g, c: (b, g, 0, 0))),
        scratch_shapes=[pltpu.VMEM((nstate, gw), F32), pltpu.VMEM((chunk, gw), F32)],
        compiler_params=_params("parallel", "parallel", "arbitrary"),
        name="ssd_scan",
    )(xbc, xbc, xbc, zx, dt_raw, vec(dt_bias), vec(a_log), vec(d_skip),
      norm_w.astype(F32).reshape(1, d_inner), h0g)
    return y, hout.reshape(bsz, nheads, hd, nstate)


def _cumsum_kernel(lf_ref, c_ref, ct_ref, carry_ref, *, tb):
    @pl.when(pl.program_id(1) == 0)
    def _():
        carry_ref[...] = jnp.zeros_like(carry_ref)

    row = lax.broadcasted_iota(jnp.int32, (tb, tb), 0)
    col = lax.broadcasted_iota(jnp.int32, (tb, tb), 1)
    cs = _dot01(row >= col, lf_ref[...]) + carry_ref[0:1, :]
    c_ref[...] = cs
    ct_ref[...] = cs.T
    carry_ref[...] = jnp.broadcast_to(cs[tb - 1:tb, :], carry_ref.shape)


def _cumsum_time(lf, tb):
    bsz, t, w = lf.shape
    return pl.pallas_call(
        functools.partial(_cumsum_kernel, tb=tb),
        out_shape=(jax.ShapeDtypeStruct((bsz, t, w), F32), jax.ShapeDtypeStruct((bsz, w, t), F32)),
        grid=(bsz, t // tb),
        in_specs=[pl.BlockSpec((None, tb, w), lambda b, i: (b, i, 0))],
        out_specs=(pl.BlockSpec((None, tb, w), lambda b, i: (b, i, 0)),
                   pl.BlockSpec((None, w, tb), lambda b, i: (b, 0, i))),
        scratch_shapes=[pltpu.VMEM((SUBLANES, w), F32)],
        compiler_params=_params("parallel", "arbitrary"),
        name="logf_cumsum",
    )(lf)


def _head_column(c_blk, h, width):
    rolled = pltpu.roll(c_blk, lax.rem(LANES - h, LANES), axis=1)
    return jnp.broadcast_to(rolled[:, 0:1], (c_blk.shape[0], width))


def _fox_prefill_kernel(q_ref, k_ref, v_ref, cq_ref, ckt_ref, o_ref, m_ref, l_ref, acc_ref, cqc_ref,
                        *, tq, tk, scale):
    h = pl.program_id(1)
    qi = pl.program_id(2)
    ki = pl.program_id(3)
    kmax = ((qi + 1) * tq - 1) // tk

    @pl.when(ki == 0)
    def _():
        m_ref[...] = jnp.full_like(m_ref, NEG_INF)
        l_ref[...] = jnp.zeros_like(l_ref)
        acc_ref[...] = jnp.zeros_like(acc_ref)
        cqc_ref[...] = _head_column(cq_ref[...], h, tk)

    @pl.when(ki <= kmax)
    def _():
        s = lax.dot_general(q_ref[...], k_ref[...].astype(MXU_DTYPE), (((1,), (1,)), ((), ())),
                            preferred_element_type=F32) * scale
        s = s + cqc_ref[...] - ckt_ref[pl.ds(h, 1), :]
        qpos = qi * tq + lax.broadcasted_iota(jnp.int32, (tq, tk), 0)
        kpos = ki * tk + lax.broadcasted_iota(jnp.int32, (tq, tk), 1)
        s = jnp.where(kpos <= qpos, s, NEG_INF)
        m_prev = m_ref[...]
        m_new = jnp.maximum(m_prev, jnp.max(s, axis=-1, keepdims=True))
        p = jnp.exp(s - m_new)
        alpha = jnp.exp(m_prev - m_new)
        l_ref[...] = alpha * l_ref[...] + jnp.sum(p, axis=-1, keepdims=True)
        acc_ref[...] = alpha * acc_ref[...] + jnp.dot(
            p.astype(MXU_DTYPE), v_ref[...].astype(MXU_DTYPE), preferred_element_type=F32)
        m_ref[...] = m_new

    @pl.when(ki == pl.num_programs(3) - 1)
    def _():
        o_ref[...] = (acc_ref[...] / l_ref[...]).astype(o_ref.dtype)


def _fox_prefill(q, k, v, c, ct, *, dh):
    bsz, t, d = q.shape
    nh = d // dh
    tq = _tile(t, 1024, 16)
    tk = tq
    nq, nk = t // tq, t // tk
    kclamp = lambda qi, ki: jnp.minimum(ki, ((qi + 1) * tq - 1) // tk)
    return pl.pallas_call(
        functools.partial(_fox_prefill_kernel, tq=tq, tk=tk, scale=dh ** -0.5),
        out_shape=jax.ShapeDtypeStruct((bsz, t, d), MXU_DTYPE),
        grid=(bsz, nh, nq, nk),
        in_specs=[
            pl.BlockSpec((None, tq, dh), lambda b, h, qi, ki: (b, qi, h)),
            pl.BlockSpec((None, tk, dh), lambda b, h, qi, ki: (b, kclamp(qi, ki), h)),
            pl.BlockSpec((None, tk, dh), lambda b, h, qi, ki: (b, kclamp(qi, ki), h)),
            pl.BlockSpec((None, tq, LANES), lambda b, h, qi, ki: (b, qi, 0)),
            pl.BlockSpec((None, LANES, tk), lambda b, h, qi, ki: (b, 0, kclamp(qi, ki))),
        ],
        out_specs=pl.BlockSpec((None, tq, dh), lambda b, h, qi, ki: (b, qi, h)),
        scratch_shapes=[pltpu.VMEM((tq, 1), F32), pltpu.VMEM((tq, 1), F32),
                        pltpu.VMEM((tq, dh), F32), pltpu.VMEM((tq, tk), F32)],
        compiler_params=_params("parallel", "parallel", "parallel", "arbitrary"),
        name="fox_prefill",
    )(q, k, v, c, ct)


def _fox_decode_kernel(q_ref, kp_ref, vp_ref, kn_ref, vn_ref, cq_ref, ckt_ref, o_ref, *, past, tq, scale):
    h = pl.program_id(1)
    nt = (((1,), (1,)), ((), ()))
    q = q_ref[...]
    ck = ckt_ref[pl.ds(h, 1), :]
    cq_p = _head_column(cq_ref[...], h, past)
    s_p = lax.dot_general(q, kp_ref[...].astype(MXU_DTYPE), nt, preferred_element_type=F32) * scale
    s_p = s_p + cq_p - ck[:, 0:past]
    s_n = lax.dot_general(q, kn_ref[...].astype(MXU_DTYPE), nt, preferred_element_type=F32) * scale
    s_n = s_n + cq_p[:, 0:tq] - ck[:, past:past + tq]
    row = lax.broadcasted_iota(jnp.int32, (tq, tq), 0)
    col = lax.broadcasted_iota(jnp.int32, (tq, tq), 1)
    s_n = jnp.where(col <= row, s_n, NEG_INF)
    m = jnp.maximum(jnp.max(s_p, axis=-1, keepdims=True), jnp.max(s_n, axis=-1, keepdims=True))
    p_p = jnp.exp(s_p - m)
    p_n = jnp.exp(s_n - m)
    l = jnp.sum(p_p, axis=-1, keepdims=True) + jnp.sum(p_n, axis=-1, keepdims=True)
    acc = jnp.dot(p_p.astype(MXU_DTYPE), vp_ref[...].astype(MXU_DTYPE), preferred_element_type=F32)
    acc = acc + jnp.dot(p_n.astype(MXU_DTYPE), vn_ref[...].astype(MXU_DTYPE), preferred_element_type=F32)
    o_ref[...] = (acc / l).astype(o_ref.dtype)


def _fox_decode(q, k_past, v_past, k_new, v_new, c, ct, *, dh):
    bsz, t, d = q.shape
    past = k_past.shape[1]
    nh = d // dh
    tpad = ct.shape[-1]
    assert past % t == 0 and past % LANES == 0
    return pl.pallas_call(
        functools.partial(_fox_decode_kernel, past=past, tq=t, scale=dh ** -0.5),
        out_shape=jax.ShapeDtypeStruct((bsz, t, d), MXU_DTYPE),
        grid=(bsz, nh),
        in_specs=[
            pl.BlockSpec((None, t, dh), lambda b, h: (b, 0, h)),
            pl.BlockSpec((None, past, dh), lambda b, h: (b, 0, h)),
            pl.BlockSpec((None, past, dh), lambda b, h: (b, 0, h)),
            pl.BlockSpec((None, t, dh), lambda b, h: (b, 0, h)),
            pl.BlockSpec((None, t, dh), lambda b, h: (b, 0, h)),
            pl.BlockSpec((None, t, LANES), lambda b, h: (b, past // t, 0)),
            pl.BlockSpec((None, LANES, tpad), lambda b, h: (b, 0, 0)),
        ],
        out_specs=pl.BlockSpec((None, t, dh), lambda b, h: (b, 0, h)),
        compiler_params=_params("parallel", "parallel"),
        name="fox_decode",
    )(q, k_past, v_past, k_new, v_new, c, ct)


def _trunk(x, mods, ssm0, conv0, ffn0, k_past, v_past, lf_past, p):
    bsz, t, d = x.shape
    m = bsz * t
    depth = p["w_ada"].shape[0]
    n_a = p["m_w_in"].shape[0]
    d_inner = p["m_w_out"].shape[1]
    conv_dim = p["m_conv_w"].shape[2]
    nheads = p["m_dt_bias"].shape[1]
    nstate = ssm0.shape[-1]
    ngroups = (conv_dim - d_inner) // (2 * nstate)
    d_ff = p["f_w_down"].shape[1]
    nh_fox, dh = k_past.shape[2], k_past.shape[3]
    past = k_past.shape[1]
    chunk = min(SSD_CHUNK, t)
    big = m >= 1024

    def gate_rows(gate):
        return gate.reshape(1, d) if bsz == 1 else jnp.broadcast_to(gate, (bsz, t, d)).reshape(m, d)

    new_ssm, new_conv, new_ffn = [], [], []
    k_new = v_new = lf_new = c_row = c_t = None
    for l in range(depth):
        shift, scale, gate = mods[2 * l]
        h = _norm(x, scale, shift, plus_one=True, out_dtype=MXU_DTYPE).reshape(m, d)
        if l < n_a:
            w_in = p["m_w_in"][l]
            zx = _matmul(h, w_in, n=d_inner + conv_dim, bm=2048 if big else 128, bn=512)
            dt_raw = _matmul(h, w_in, n_off=d_inner + conv_dim, n=nheads, bm=2048 if big else 128,
                             bn=nheads)
            zx = zx.reshape(bsz, t, d_inner + conv_dim)
            xbc, tail = _causal_conv(zx, conv0[l], p["m_conv_w"][l], p["m_conv_b"][l],
                                     col_off=d_inner, cdim=conv_dim, tc=2048, tt=256, out_dtype=F32)
            new_conv.append(tail[:, SUBLANES - (p["m_conv_w"].shape[1] - 1):])
            y, sst = _ssd(xbc, zx, dt_raw.reshape(bsz, t, nheads), p["m_dt_bias"][l], p["m_a_log"][l],
                          p["m_d"][l], p["m_norm_w"][l], ssm0[l], d_inner=d_inner, ngroups=ngroups,
                          nstate=nstate, chunk=chunk)
            new_ssm.append(sst)
            x = _matmul(y.reshape(m, d_inner), p["m_w_out"][l], bm=1024 if big else 128, bn=512, bk=2048,
                        epilogue="resid", res=x.reshape(m, d), gate=gate_rows(gate)).reshape(bsz, t, d)
        else:
            j = l - n_a
            q = _matmul(h, p["w_q"][j], out_dtype=MXU_DTYPE, bm=2048 if big else 128, bn=512)
            q = q.reshape(bsz, t, d)
            if past == 0:
                o = _fox_prefill(q, k_new, v_new, c_row, c_t, dh=dh)
            else:
                o = _fox_decode(q, k_past.reshape(bsz, past, d), v_past.reshape(bsz, past, d),
                                k_new, v_new, c_row, c_t, dh=dh)
            x = _matmul(o.reshape(m, d), p["w_o"][j], bm=1024 if big else 128, bn=512,
                        epilogue="resid", res=x.reshape(m, d), gate=gate_rows(gate)).reshape(bsz, t, d)
        shift, scale, gate = mods[2 * l + 1]
        h = _norm(x, scale, shift, plus_one=True, out_dtype=MXU_DTYPE).reshape(m, d)
        up = _matmul(h, p["f_w_up"][l], bm=2048 if big else 128, bn=512).reshape(bsz, t, 2 * d_ff)
        u, tail = _causal_conv(up, ffn0[l], p["f_conv_w"][l], p["f_conv_b"][l], col_off=d_ff, cdim=d_ff,
                               tc=d_ff, tt=64, out_dtype=MXU_DTYPE, gate_col_off=0)
        new_ffn.append(tail[:, SUBLANES - (p["f_conv_w"].shape[1] - 1):])
        x = _matmul(u.reshape(m, d_ff), p["f_w_down"][l], bm=1024 if big else 128, bn=256, bk=d_ff // 2,
                    epilogue="resid", res=x.reshape(m, d), gate=gate_rows(gate)).reshape(bsz, t, d)
        if l == n_a - 1:
            hkv = _norm(x, p["kv_norm_w"].reshape(1, 1, d), plus_one=False, out_dtype=MXU_DTYPE)
            hkv = hkv.reshape(m, d)
            k_new = _matmul(hkv, p["w_kv"], n=d, bm=2048 if big else 128, bn=512).reshape(bsz, t, d)
            v_new = _matmul(hkv, p["w_kv"], n_off=d, n=d, bm=2048 if big else 128, bn=512)
            v_new = v_new.reshape(bsz, t, d)
            w_fg = jnp.pad(p["w_fgate"], ((0, 0), (0, LANES - nh_fox)))
            b_fg = jnp.pad(p["b_fgate"].astype(F32), (0, LANES - nh_fox)).reshape(1, LANES)
            lf_pad = _matmul(hkv, w_fg, bm=2048 if big else 128, bn=LANES, epilogue="logsig", bias=b_fg)
            lf_pad = lf_pad.reshape(bsz, t, LANES)
            lf_new = lf_pad[:, :, :nh_fox]
            tb = 256
            lf_all = lf_pad
            if past > 0:
                lf_p = jnp.pad(lf_past.astype(F32), ((0, 0), (0, 0), (0, LANES - nh_fox)))
                lf_all = jnp.concatenate([lf_p, lf_pad], axis=1)
            tot = lf_all.shape[1]
            lf_all = jnp.pad(lf_all, ((0, 0), (0, (-tot) % tb), (0, 0)))
            c_row, c_t = _cumsum_time(lf_all, tb)
    y_out = _norm(x, p["final_norm_w"].reshape(1, 1, d), plus_one=False, out_dtype=x.dtype)
    return (y_out, jnp.stack(new_ssm).astype(x.dtype), jnp.stack(new_conv), jnp.stack(new_ffn),
            k_new.reshape(bsz, t, nh_fox, dh), v_new.reshape(bsz, t, nh_fox, dh), lf_new.astype(x.dtype))


def kernel(x_prompt, x_sample, c_prompt, c_sample, cache_k, cache_v, cache_logf, state_ssm, state_conv, state_ffn_conv, w_ada, b_ada, m_w_in, m_conv_w, m_conv_b, m_dt_bias, m_a_log, m_d, m_norm_w, m_w_out, kv_norm_w, w_kv, w_fgate, b_fgate, w_q, w_o, f_w_up, f_conv_w, f_conv_b, f_w_down, final_norm_w):
    p = dict(w_ada=w_ada, b_ada=b_ada, m_w_in=m_w_in, m_conv_w=m_conv_w, m_conv_b=m_conv_b,
             m_dt_bias=m_dt_bias, m_a_log=m_a_log, m_d=m_d, m_norm_w=m_norm_w, m_w_out=m_w_out,
             kv_norm_w=kv_norm_w, w_kv=w_kv, w_fgate=w_fgate, b_fgate=b_fgate, w_q=w_q, w_o=w_o,
             f_w_up=f_w_up, f_conv_w=f_conv_w, f_conv_b=f_conv_b, f_w_down=f_w_down,
             final_norm_w=final_norm_w)
    bp, d = c_prompt.shape
    bs = c_sample.shape[0]
    n_a, depth = m_w_in.shape[0], w_ada.shape[0]
    nheads, nstate = state_ssm.shape[2], state_ssm.shape[4]
    hd = state_ssm.shape[3]
    dtp = x_prompt.dtype

    rows = bp + bs
    c_rows = jnp.pad(jnp.concatenate([c_prompt, c_sample], axis=0), ((0, (-rows) % 16), (0, 0)))
    mod = _ada_all(c_rows, w_ada, b_ada)

    def mods_for(lo, hi):
        return [tuple(mod[s, lo:hi, i * d:(i + 1) * d][:, None, :] for i in range(3))
                for s in range(2 * depth)]

    ssm0 = jnp.zeros((n_a, bp, nheads, hd, nstate), F32)
    conv0 = jnp.zeros((n_a, bp, m_conv_w.shape[1] - 1, m_conv_w.shape[2]), dtp)
    ffn0 = jnp.zeros((depth, bp, f_conv_w.shape[1] - 1, f_conv_w.shape[2]), dtp)
    kv0 = jnp.zeros((bp, 0) + cache_k.shape[2:], dtp)
    lf0 = jnp.zeros((bp, 0, cache_logf.shape[2]), F32)
    out_p = _trunk(x_prompt, mods_for(0, bp), ssm0, conv0, ffn0, kv0, kv0, lf0, p)
    out_s = _trunk(x_sample, mods_for(bp, rows), state_ssm, state_conv, state_ffn_conv,
                   cache_k, cache_v, cache_logf, p)
    return (out_p[0], out_s[0]) + out_p[1:] + out_s[1:]
```

```python
import functools
import math

import jax
import jax.numpy as jnp
from jax import lax
from jax.experimental import pallas as pl
from jax.experimental.pallas import tpu as pltpu

F32 = jnp.float32
MXU_DTYPE = jnp.bfloat16
EPS = 1e-6
LANES = 128
SUBLANES = 8
VMEM_LIMIT_BYTES = 56 * 1024 * 1024
SSD_CHUNK = 64
NEG_INF = float("-inf")
LOG2E = math.log2(math.e)


def _tile(dim, pref, align):
    t = (min(pref, dim) // align) * align
    while t >= align:
        if dim % t == 0:
            return t
        t -= align
    return dim


def _params(*sem):
    return pltpu.CompilerParams(dimension_semantics=sem, vmem_limit_bytes=VMEM_LIMIT_BYTES)


def _silu(x):
    return x * jax.nn.sigmoid(x)


def _softplus(x):
    return jnp.maximum(x, 0.0) + jnp.log1p(jnp.exp(-jnp.abs(x)))


def _split3(a):
    a1 = a.astype(MXU_DTYPE)
    r1 = a - a1.astype(F32)
    a2 = r1.astype(MXU_DTYPE)
    r2 = r1 - a2.astype(F32)
    return a1, a2, r2.astype(MXU_DTYPE)


def _dot01(m01, a, dims=(((1,), (0,)), ((), ()))):
    m = m01.astype(MXU_DTYPE)
    out = None
    for t in _split3(a):
        part = lax.dot_general(m, t, dims, preferred_element_type=F32)
        out = part if out is None else out + part
    return out


def _ada_kernel(c_ref, w_ref, b_ref, o_ref):
    a = _silu(c_ref[...]).astype(MXU_DTYPE)
    w = w_ref[...].astype(MXU_DTYPE)
    o_ref[...] = jnp.dot(a, w, preferred_element_type=F32) + b_ref[...]


def _ada_all(c_rows, w_ada, b_ada):
    r, d = c_rows.shape
    s = w_ada.shape[0] * w_ada.shape[1]
    n = w_ada.shape[-1]
    w = w_ada.reshape(s, d, n)
    b = b_ada.reshape(s, 1, n)
    tn = _tile(n, 512, LANES)
    return pl.pallas_call(
        _ada_kernel,
        out_shape=jax.ShapeDtypeStruct((s, r, n), F32),
        grid=(s, n // tn),
        in_specs=[
            pl.BlockSpec((r, d), lambda i, j: (0, 0)),
            pl.BlockSpec((None, d, tn), lambda i, j: (i, 0, j)),
            pl.BlockSpec((None, 1, tn), lambda i, j: (i, 0, j)),
        ],
        out_specs=pl.BlockSpec((None, r, tn), lambda i, j: (i, 0, j)),
        compiler_params=_params("parallel", "parallel"),
        name="ada_mod",
    )(c_rows, w, b)


def _norm_kernel(*refs, plus_one, has_add):
    x_ref, mul_ref = refs[0], refs[1]
    add_ref = refs[2] if has_add else None
    o_ref = refs[-1]
    x = x_ref[...]
    y = x * lax.rsqrt(jnp.mean(x * x, axis=-1, keepdims=True) + EPS)
    m = mul_ref[...]
    if plus_one:
        m = 1.0 + m
    y = y * m
    if has_add:
        y = y + add_ref[...]
    o_ref[...] = y.astype(o_ref.dtype)


def _norm(x, mul, add=None, *, plus_one, out_dtype):
    b, t, d = x.shape
    tt = _tile(t, 256, 16)
    per_batch = mul.shape[0] == b and b > 1
    mod_map = (lambda i, j: (i, 0, 0)) if per_batch else (lambda i, j: (0, 0, 0))
    ins = [x, mul]
    specs = [pl.BlockSpec((None, tt, d), lambda i, j: (i, j, 0)),
             pl.BlockSpec((None, 1, d), mod_map)]
    if add is not None:
        ins.append(add)
        specs.append(pl.BlockSpec((None, 1, d), mod_map))
    return pl.pallas_call(
        functools.partial(_norm_kernel, plus_one=plus_one, has_add=add is not None),
        out_shape=jax.ShapeDtypeStruct((b, t, d), out_dtype),
        grid=(b, t // tt),
        in_specs=specs,
        out_specs=pl.BlockSpec((None, tt, d), lambda i, j: (i, j, 0)),
        compiler_params=_params("parallel", "parallel"),
        name="rmsnorm_mod",
    )(*ins)


def _mm_kernel(*refs, nk, epilogue, out_scale):
    x_ref, w_ref = refs[0], refs[1]
    pos = 2
    if epilogue == "resid":
        res_ref, gate_ref = refs[2], refs[3]
        pos = 4
    elif epilogue == "logsig":
        bias_ref = refs[2]
        pos = 3
    o_ref = refs[pos]
    part = jnp.dot(x_ref[...], w_ref[...].astype(MXU_DTYPE), preferred_element_type=F32)

    def finish(acc):
        if out_scale is not None:
            acc = acc * out_scale
        if epilogue == "resid":
            acc = res_ref[...] + gate_ref[...] * acc
        elif epilogue == "logsig":
            acc = -_softplus(-(acc + bias_ref[...]))
        o_ref[...] = acc.astype(o_ref.dtype)

    if nk == 1:
        finish(part)
    else:
        acc_ref = refs[pos + 1]
        k = pl.program_id(2)

        @pl.when(k == 0)
        def _():
            acc_ref[...] = part

        @pl.when(k > 0)
        def _():
            acc_ref[...] += part

        @pl.when(k == nk - 1)
        def _():
            finish(acc_ref[...])


def _matmul(x, w, *, layer=0, n_off=0, n=None, out_dtype=F32, bm=1024, bn=512, bk=4096,
            epilogue=None, res=None, gate=None, bias=None, out_scale=None):
    m, kdim = x.shape
    if w.ndim == 2:
        w = w[None]
    n = w.shape[2] if n is None else n
    bm = _tile(m, bm, 16)
    bn = _tile(math.gcd(n, n_off), bn, LANES)
    bk = _tile(kdim, bk, LANES)
    joff = n_off // bn
    nk = kdim // bk
    ins = [x, w]
    x_mode = dict(pipeline_mode=pl.Buffered(1)) if nk == 1 and n // bn > 1 else {}
    specs = [pl.BlockSpec((bm, bk), lambda i, j, k: (i, k), **x_mode),
             pl.BlockSpec((None, bk, bn), lambda i, j, k: (layer, k, j + joff))]
    if epilogue == "resid":
        ins += [res, gate]
        specs.append(pl.BlockSpec((bm, bn), lambda i, j, k: (i, j)))
        if gate.shape[0] == 1:
            specs.append(pl.BlockSpec((1, bn), lambda i, j, k: (0, j)))
        else:
            specs.append(pl.BlockSpec((bm, bn), lambda i, j, k: (i, j)))
    elif epilogue == "logsig":
        ins.append(bias)
        specs.append(pl.BlockSpec((1, bn), lambda i, j, k: (0, j)))
    scratch = [pltpu.VMEM((bm, bn), F32)] if nk > 1 else []
    return pl.pallas_call(
        functools.partial(_mm_kernel, nk=nk, epilogue=epilogue, out_scale=out_scale),
        out_shape=jax.ShapeDtypeStruct((m, n), out_dtype),
        grid=(m // bm, n // bn, nk),
        in_specs=specs,
        out_specs=pl.BlockSpec((bm, bn), lambda i, j, k: (i, j)),
        scratch_shapes=scratch,
        compiler_params=_params("parallel", "parallel", "arbitrary"),
        name="matmul_" + (epilogue or "plain"),
    )(*ins)


def _conv_kernel(*refs, width, tt, gated):
    if gated:
        a_ref, x_ref, buf_ref, w_ref, b_ref, y_ref, tail_ref, xx_ref = refs
    else:
        x_ref, buf_ref, w_ref, b_ref, y_ref, tail_ref, xx_ref = refs
    ti = pl.program_id(2)

    @pl.when(ti == 0)
    def _():
        xx_ref[0:SUBLANES, :] = buf_ref[...]

    xx_ref[SUBLANES:SUBLANES + tt, :] = x_ref[...]
    acc = b_ref[...]
    for k in range(width):
        lo = SUBLANES - (width - 1) + k
        acc = acc + w_ref[k:k + 1, :] * xx_ref[lo:lo + tt, :]
    y = _silu(acc)
    if gated:
        y = y * a_ref[...]
    y_ref[...] = y.astype(y_ref.dtype)
    tail = xx_ref[tt:tt + SUBLANES, :]
    xx_ref[0:SUBLANES, :] = tail
    tail_ref[...] = tail


def _causal_conv(src, buf, w, b, *, col_off, cdim, tc, tt, out_dtype, gate_col_off=None):
    bsz, t, _ = src.shape
    width = w.shape[0]
    assert t >= SUBLANES and width - 1 <= SUBLANES
    tt = _tile(t, tt, SUBLANES)
    tc = _tile(math.gcd(cdim, col_off, gate_col_off or 0), tc, LANES)
    assert col_off % tc == 0
    coff = col_off // tc
    buf8 = jnp.pad(buf.astype(F32), ((0, 0), (SUBLANES - (width - 1), 0), (0, 0)))
    w8 = jnp.pad(w, ((0, SUBLANES - width), (0, 0)))
    b2 = b.reshape(1, cdim)
    gated = gate_col_off is not None
    ins, specs = [], []
    if gated:
        assert gate_col_off % tc == 0
        goff = gate_col_off // tc
        ins.append(src)
        specs.append(pl.BlockSpec((None, tt, tc), lambda i, j, k: (i, k, j + goff)))
    ins += [src, buf8, w8, b2]
    specs += [
        pl.BlockSpec((None, tt, tc), lambda i, j, k: (i, k, j + coff)),
        pl.BlockSpec((None, SUBLANES, tc), lambda i, j, k: (i, 0, j)),
        pl.BlockSpec((SUBLANES, tc), lambda i, j, k: (0, j)),
        pl.BlockSpec((1, tc), lambda i, j, k: (0, j)),
    ]
    return pl.pallas_call(
        functools.partial(_conv_kernel, width=width, tt=tt, gated=gated),
        out_shape=(jax.ShapeDtypeStruct((bsz, t, cdim), out_dtype),
                   jax.ShapeDtypeStruct((bsz, SUBLANES, cdim), F32)),
        grid=(bsz, cdim // tc, t // tt),
        in_specs=specs,
        out_specs=(pl.BlockSpec((None, tt, tc), lambda i, j, k: (i, k, j)),
                   pl.BlockSpec((None, SUBLANES, tc), lambda i, j, k: (i, 0, j))),
        scratch_shapes=[pltpu.VMEM((tt + SUBLANES, tc), F32)],
        compiler_params=_params("parallel", "parallel", "arbitrary"),
        name="causal_conv_gated" if gated else "causal_conv",
    )(*ins)


def _ssd_kernel(x_ref, b_ref, c_ref, z_ref, dt_ref, dtb_ref, alog_ref, dsk_ref, nw_ref, h0_ref,
                y_ref, hout_ref, ht_ref, yg_ref, *, hpg, hd, ln, nheads):
    g = pl.program_id(1)
    c = pl.program_id(2)
    gw = hpg * hd
    per_blk = LANES // hd
    nblk = gw // LANES

    @pl.when(c == 0)
    def _():
        ht_ref[...] = h0_ref[...].T

    dt = _softplus(dt_ref[...] + dtb_ref[...])
    a_neg = -jnp.exp(alog_ref[...])
    shift = lax.rem(nheads - g * hpg, nheads)
    dt_g = pltpu.roll(dt, shift, axis=1)
    da_g = pltpu.roll(dt * a_neg, shift, axis=1)
    dsk_g = pltpu.roll(jnp.broadcast_to(dsk_ref[...], (SUBLANES, nheads)), shift, axis=1)

    row = lax.broadcasted_iota(jnp.int32, (ln, ln), 0)
    col = lax.broadcasted_iota(jnp.int32, (ln, ln), 1)
    tril = row >= col
    a_cum = _dot01(tril, da_g)
    hp = max(hpg, SUBLANES)
    sel = (lax.broadcasted_iota(jnp.int32, (hp, nheads), 0)
           == lax.broadcasted_iota(jnp.int32, (hp, nheads), 1))
    nt = (((1,), (1,)), ((), ()))
    a_cum_t = _dot01(sel, a_cum, nt)

    bmat = b_ref[...].astype(MXU_DTYPE)
    cmat = c_ref[...].astype(MXU_DTYPE)
    cb = lax.dot_general(cmat, bmat, nt, preferred_element_type=F32)

    lane = lax.broadcasted_iota(jnp.int32, (1, LANES), 1)
    ssq = jnp.zeros((ln, 1), F32)
    for j in range(nblk):
        lo, hi = j * LANES, (j + 1) * LANES
        x_blk = x_ref[:, lo:hi]
        acol = dcol = dsk_row = None
        for w in range(per_blk):
            i = j * per_blk + w
            a_b = jnp.broadcast_to(a_cum[:, i:i + 1], (ln, LANES))
            d_b = jnp.broadcast_to(dt_g[:, i:i + 1], (ln, LANES))
            s_b = jnp.broadcast_to(dsk_g[0:1, i:i + 1], (1, LANES))
            if w == 0:
                acol, dcol, dsk_row = a_b, d_b, s_b
            else:
                in_w = lane >= w * hd
                acol = jnp.where(in_w, a_b, acol)
                dcol = jnp.where(in_w, d_b, dcol)
                dsk_row = jnp.where(in_w, s_b, dsk_row)
        alast = acol[ln - 1:ln, :]
        xdt = x_blk * dcol
        xdt_m = xdt.astype(MXU_DTYPE)
        xw_m = (xdt * jnp.exp(alast - acol)).astype(MXU_DTYPE)
        ydiag = None
        for w in range(per_blk):
            i = j * per_blk + w
            seg = acol[:, w * hd:w * hd + 1] - a_cum_t[i:i + 1, :]
            dec = jnp.exp(jnp.where(tril, seg, NEG_INF))
            yw = jnp.dot((cb * dec).astype(MXU_DTYPE), xdt_m, preferred_element_type=F32)
            ydiag = yw if w == 0 else jnp.where(lane >= w * hd, yw, ydiag)
        h_blk = ht_ref[:, lo:hi]
        yoff = jnp.dot(cmat, h_blk.astype(MXU_DTYPE), preferred_element_type=F32) * jnp.exp(acol)
        st = lax.dot_general(bmat, xw_m, (((0,), (0,)), ((), ())), preferred_element_type=F32)
        ht_ref[:, lo:hi] = h_blk * jnp.exp(alast) + st
        y = ydiag + yoff + dsk_row * x_blk
        y = y * _silu(z_ref[:, lo:hi])
        yg_ref[:, lo:hi] = y
        ssq = ssq + jnp.sum(y * y, axis=-1, keepdims=True)

    inv = lax.rsqrt(ssq * (1.0 / gw) + EPS)
    y_ref[...] = (yg_ref[...] * inv * nw_ref[...]).astype(y_ref.dtype)

    @pl.when(c == pl.num_programs(2) - 1)
    def _():
        hout_ref[...] = ht_ref[...].T


def _ssd(xbc, zx, dt_raw, dt_bias, a_log, d_skip, norm_w, h0, *, d_inner, ngroups, nstate, chunk):
    bsz, t, _ = xbc.shape
    nheads = dt_raw.shape[-1]
    hd = d_inner // nheads
    hpg = nheads // ngroups
    gw = hpg * hd
    assert nheads <= LANES
    assert LANES % hd == 0 and gw % LANES == 0 and nstate % LANES == 0 and t % chunk == 0
    nc = t // chunk
    h0g = h0.astype(F32).reshape(bsz, ngroups, gw, nstate)
    vec = lambda a: a.astype(F32).reshape(1, nheads)
    xoff = d_inner // nstate
    kern = functools.partial(_ssd_kernel, hpg=hpg, hd=hd, ln=chunk, nheads=nheads)
    y, hout = pl.pallas_call(
        kern,
        out_shape=(jax.ShapeDtypeStruct((bsz, t, d_inner), MXU_DTYPE),
                   jax.ShapeDtypeStruct((bsz, ngroups, gw, nstate), F32)),
        grid=(bsz, ngroups, nc),
        in_specs=[
            pl.BlockSpec((None, chunk, gw), lambda b, g, c: (b, c, g)),
            pl.BlockSpec((None, chunk, nstate), lambda b, g, c: (b, c, xoff + g)),
            pl.BlockSpec((None, chunk, nstate), lambda b, g, c: (b, c, xoff + ngroups + g)),
            pl.BlockSpec((None, chunk, gw), lambda b, g, c: (b, c, g)),
            pl.BlockSpec((None, chunk, nheads), lambda b, g, c: (b, c, 0)),
            pl.BlockSpec((1, nheads), lambda b, g, c: (0, 0)),
            pl.BlockSpec((1, nheads), lambda b, g, c: (0, 0)),
            pl.BlockSpec((1, nheads), lambda b, g, c: (0, 0)),
            pl.BlockSpec((1, gw), lambda b, g, c: (0, g)),
            pl.BlockSpec((None, None, gw, nstate), lambda b, g, c: (b, g, 0, 0)),
        ],
        out_specs=(pl.BlockSpec((None, chunk, gw), lambda b, g, c: (b, c, g)),
                   pl.BlockSpec((None, None, gw, nstate), lambda b, g, c: (b, g, 0, 0))),
        scratch_shapes=[pltpu.VMEM((nstate, gw), F32), pltpu.VMEM((chunk, gw), F32)],
        compiler_params=_params("parallel", "parallel", "arbitrary"),
        name="ssd_scan",
    )(xbc, xbc, xbc, zx, dt_raw, vec(dt_bias), vec(a_log), vec(d_skip),
      norm_w.astype(F32).reshape(1, d_inner), h0g)
    return y, hout.reshape(bsz, nheads, hd, nstate)


def _cumsum_kernel(lf_ref, c_ref, ct_ref, carry_ref, *, tb):
    @pl.when(pl.program_id(1) == 0)
    def _():
        carry_ref[...] = jnp.zeros_like(carry_ref)

    row = lax.broadcasted_iota(jnp.int32, (tb, tb), 0)
    col = lax.broadcasted_iota(jnp.int32, (tb, tb), 1)
    cs = _dot01(row >= col, lf_ref[...]) + carry_ref[0:1, :]
    c_ref[...] = cs
    ct_ref[...] = cs.T
    carry_ref[...] = jnp.broadcast_to(cs[tb - 1:tb, :], carry_ref.shape)


def _cumsum_time(lf, tb):
    bsz, t, w = lf.shape
    return pl.pallas_call(
        functools.partial(_cumsum_kernel, tb=tb),
        out_shape=(jax.ShapeDtypeStruct((bsz, t, w), F32), jax.ShapeDtypeStruct((bsz, w, t), F32)),
        grid=(bsz, t // tb),
        in_specs=[pl.BlockSpec((None, tb, w), lambda b, i: (b, i, 0))],
        out_specs=(pl.BlockSpec((None, tb, w), lambda b, i: (b, i, 0)),
                   pl.BlockSpec((None, w, tb), lambda b, i: (b, 0, i))),
        scratch_shapes=[pltpu.VMEM((SUBLANES, w), F32)],
        compiler_params=_params("parallel", "arbitrary"),
        name="logf_cumsum",
    )(lf)


def _head_column(c_blk, h, width):
    rolled = pltpu.roll(c_blk, lax.rem(LANES - h, LANES), axis=1)
    return jnp.broadcast_to(rolled[:, 0:1], (c_blk.shape[0], width))


def _aug_cols(c_blk, h, for_query):
    col = _head_column(c_blk, h, LANES) * LOG2E
    c1, c2, c3 = (t.astype(F32) for t in _split3(col))
    one = jnp.ones_like(col)
    terms = (c1, c2, c3, one, one, one) if for_query else (one, one, one, -c1, -c2, -c3)
    lane = lax.broadcasted_iota(jnp.int32, col.shape, 1)
    out = jnp.zeros_like(col)
    for idx, term in enumerate(terms):
        out = jnp.where(lane == idx, term, out)
    return out.astype(MXU_DTYPE)


def _fox_prefill_kernel(q_ref, k_ref, v_ref, cq_ref, ck_ref, o_ref, m_ref, l_ref, acc_ref, qa_ref, *, tq):
    h = pl.program_id(1)
    qi = pl.program_id(2)
    ki = pl.program_id(3)
    dh = q_ref.shape[-1]

    @pl.when(ki == 0)
    def _():
        m_ref[...] = jnp.full_like(m_ref, NEG_INF)
        l_ref[...] = jnp.zeros_like(l_ref)
        acc_ref[...] = jnp.zeros_like(acc_ref)
        qa_ref[:, 0:dh] = q_ref[...]
        qa_ref[:, dh:dh + LANES] = _aug_cols(cq_ref[...], h, True)

    def step(diagonal):
        ka = jnp.concatenate([k_ref[...].astype(MXU_DTYPE), _aug_cols(ck_ref[...], h, False)], axis=1)
        s = lax.dot_general(qa_ref[...], ka, (((1,), (1,)), ((), ())), preferred_element_type=F32)
        if diagonal:
            row = lax.broadcasted_iota(jnp.int32, (tq, tq), 0)
            col = lax.broadcasted_iota(jnp.int32, (tq, tq), 1)
            s = jnp.where(col <= row, s, NEG_INF)
        m_prev = m_ref[...]
        m_new = jnp.maximum(m_prev, jnp.max(s, axis=-1, keepdims=True))
        p = jnp.exp2(s - m_new)
        alpha = jnp.exp2(m_prev - m_new)
        l_ref[...] = alpha * l_ref[...] + jnp.sum(p, axis=-1, keepdims=True)
        acc_ref[...] = alpha * acc_ref[...] + jnp.dot(
            p.astype(MXU_DTYPE), v_ref[...].astype(MXU_DTYPE), preferred_element_type=F32)
        m_ref[...] = m_new

    @pl.when(ki < qi)
    def _():
        step(False)

    @pl.when(ki == qi)
    def _():
        step(True)

    @pl.when(ki == pl.num_programs(3) - 1)
    def _():
        o_ref[...] = (acc_ref[...] / l_ref[...]).astype(o_ref.dtype)


def _fox_prefill(q, k, v, c, *, dh):
    bsz, t, d = q.shape
    nh = d // dh
    tq = _tile(t, 1024, 16)
    nq = t // tq
    kclamp = lambda qi, ki: jnp.minimum(ki, qi)
    return pl.pallas_call(
        functools.partial(_fox_prefill_kernel, tq=tq),
        out_shape=jax.ShapeDtypeStruct((bsz, t, d), MXU_DTYPE),
        grid=(bsz, nh, nq, nq),
        in_specs=[
            pl.BlockSpec((None, tq, dh), lambda b, h, qi, ki: (b, qi, h)),
            pl.BlockSpec((None, tq, dh), lambda b, h, qi, ki: (b, kclamp(qi, ki), h)),
            pl.BlockSpec((None, tq, dh), lambda b, h, qi, ki: (b, kclamp(qi, ki), h)),
            pl.BlockSpec((None, tq, LANES), lambda b, h, qi, ki: (b, qi, 0)),
            pl.BlockSpec((None, tq, LANES), lambda b, h, qi, ki: (b, kclamp(qi, ki), 0)),
        ],
        out_specs=pl.BlockSpec((None, tq, dh), lambda b, h, qi, ki: (b, qi, h)),
        scratch_shapes=[pltpu.VMEM((tq, 1), F32), pltpu.VMEM((tq, 1), F32),
                        pltpu.VMEM((tq, dh), F32), pltpu.VMEM((tq, dh + LANES), MXU_DTYPE)],
        compiler_params=_params("parallel", "parallel", "parallel", "arbitrary"),
        name="fox_prefill",
    )(q, k, v, c, c)


def _fox_decode_kernel(q_ref, kp_ref, vp_ref, kn_ref, vn_ref, cq_ref, ckt_ref, o_ref, *, past, tq):
    h = pl.program_id(1)
    nt = (((1,), (1,)), ((), ()))
    q = q_ref[...]
    ck = ckt_ref[pl.ds(h, 1), :]
    cq_p = _head_column(cq_ref[...], h, past)
    s_p = lax.dot_general(q, kp_ref[...].astype(MXU_DTYPE), nt, preferred_element_type=F32)
    s_p = s_p + (cq_p - ck[:, 0:past]) * LOG2E
    s_n = lax.dot_general(q, kn_ref[...].astype(MXU_DTYPE), nt, preferred_element_type=F32)
    s_n = s_n + (cq_p[:, 0:tq] - ck[:, past:past + tq]) * LOG2E
    row = lax.broadcasted_iota(jnp.int32, (tq, tq), 0)
    col = lax.broadcasted_iota(jnp.int32, (tq, tq), 1)
    s_n = jnp.where(col <= row, s_n, NEG_INF)
    m = jnp.maximum(jnp.max(s_p, axis=-1, keepdims=True), jnp.max(s_n, axis=-1, keepdims=True))
    p_p = jnp.exp2(s_p - m)
    p_n = jnp.exp2(s_n - m)
    l = jnp.sum(p_p, axis=-1, keepdims=True) + jnp.sum(p_n, axis=-1, keepdims=True)
    acc = jnp.dot(p_p.astype(MXU_DTYPE), vp_ref[...].astype(MXU_DTYPE), preferred_element_type=F32)
    acc = acc + jnp.dot(p_n.astype(MXU_DTYPE), vn_ref[...].astype(MXU_DTYPE), preferred_element_type=F32)
    o_ref[...] = (acc / l).astype(o_ref.dtype)


def _fox_decode(q, k_past, v_past, k_new, v_new, c, ct, *, dh):
    bsz, t, d = q.shape
    past = k_past.shape[1]
    nh = d // dh
    tpad = ct.shape[-1]
    assert past % t == 0 and past % LANES == 0
    return pl.pallas_call(
        functools.partial(_fox_decode_kernel, past=past, tq=t),
        out_shape=jax.ShapeDtypeStruct((bsz, t, d), MXU_DTYPE),
        grid=(bsz, nh),
        in_specs=[
            pl.BlockSpec((None, t, dh), lambda b, h: (b, 0, h)),
            pl.BlockSpec((None, past, dh), lambda b, h: (b, 0, h)),
            pl.BlockSpec((None, past, dh), lambda b, h: (b, 0, h)),
            pl.BlockSpec((None, t, dh), lambda b, h: (b, 0, h)),
            pl.BlockSpec((None, t, dh), lambda b, h: (b, 0, h)),
            pl.BlockSpec((None, t, LANES), lambda b, h: (b, past // t, 0)),
            pl.BlockSpec((None, LANES, tpad), lambda b, h: (b, 0, 0)),
        ],
        out_specs=pl.BlockSpec((None, t, dh), lambda b, h: (b, 0, h)),
        compiler_params=_params("parallel", "parallel"),
        name="fox_decode",
    )(q, k_past, v_past, k_new, v_new, c, ct)


def _trunk(x, mods, ssm0, conv0, ffn0, k_past, v_past, lf_past, p):
    bsz, t, d = x.shape
    m = bsz * t
    depth = p["w_ada"].shape[0]
    n_a = p["m_w_in"].shape[0]
    d_inner = p["m_w_out"].shape[1]
    conv_dim = p["m_conv_w"].shape[2]
    nheads = p["m_dt_bias"].shape[1]
    nstate = ssm0.shape[-1]
    ngroups = (conv_dim - d_inner) // (2 * nstate)
    d_ff = p["f_w_down"].shape[1]
    nh_fox, dh = k_past.shape[2], k_past.shape[3]
    past = k_past.shape[1]
    chunk = min(SSD_CHUNK, t)
    big = m >= 1024

    def gate_rows(gate):
        return gate.reshape(1, d) if bsz == 1 else jnp.broadcast_to(gate, (bsz, t, d)).reshape(m, d)

    new_ssm, new_conv, new_ffn = [], [], []
    k_new = v_new = lf_new = c_row = c_t = None
    for l in range(depth):
        shift, scale, gate = mods[2 * l]
        h = _norm(x, scale, shift, plus_one=True, out_dtype=MXU_DTYPE).reshape(m, d)
        if l < n_a:
            w_in = p["m_w_in"]
            zx = _matmul(h, w_in, layer=l, n=d_inner + conv_dim, bm=2048 if big else 128, bn=512)
            dt_raw = _matmul(h, w_in, layer=l, n_off=d_inner + conv_dim, n=nheads,
                             bm=2048 if big else 128, bn=nheads)
            zx = zx.reshape(bsz, t, d_inner + conv_dim)
            xbc, tail = _causal_conv(zx, conv0[l], p["m_conv_w"][l], p["m_conv_b"][l],
                                     col_off=d_inner, cdim=conv_dim, tc=2048, tt=256, out_dtype=F32)
            new_conv.append(tail[:, SUBLANES - (p["m_conv_w"].shape[1] - 1):])
            y, sst = _ssd(xbc, zx, dt_raw.reshape(bsz, t, nheads), p["m_dt_bias"][l], p["m_a_log"][l],
                          p["m_d"][l], p["m_norm_w"][l], ssm0[l], d_inner=d_inner, ngroups=ngroups,
                          nstate=nstate, chunk=chunk)
            new_ssm.append(sst)
            x = _matmul(y.reshape(m, d_inner), p["m_w_out"], layer=l, bm=1024 if big else 128, bn=256,
                        bk=d_inner, epilogue="resid", res=x.reshape(m, d),
                        gate=gate_rows(gate)).reshape(bsz, t, d)
        else:
            j = l - n_a
            q = _matmul(h, p["w_q"], layer=j, out_dtype=MXU_DTYPE, bm=2048 if big else 128, bn=512,
                        out_scale=dh ** -0.5 * LOG2E)
            q = q.reshape(bsz, t, d)
            if past == 0:
                o = _fox_prefill(q, k_new, v_new, c_row, dh=dh)
            else:
                kp = k_past.astype(MXU_DTYPE).reshape(bsz, past, d)
                vp = v_past.astype(MXU_DTYPE).reshape(bsz, past, d)
                o = _fox_decode(q, kp, vp, k_new, v_new, c_row, c_t, dh=dh)
            x = _matmul(o.reshape(m, d), p["w_o"], layer=j, bm=1024 if big else 128, bn=512,
                        epilogue="resid", res=x.reshape(m, d), gate=gate_rows(gate)).reshape(bsz, t, d)
        shift, scale, gate = mods[2 * l + 1]
        h = _norm(x, scale, shift, plus_one=True, out_dtype=MXU_DTYPE).reshape(m, d)
        up = _matmul(h, p["f_w_up"], layer=l, bm=2048 if big else 128, bn=512)
        up = up.reshape(bsz, t, 2 * d_ff)
        u, tail = _causal_conv(up, ffn0[l], p["f_conv_w"][l], p["f_conv_b"][l], col_off=d_ff, cdim=d_ff,
                               tc=d_ff, tt=64, out_dtype=MXU_DTYPE, gate_col_off=0)
        new_ffn.append(tail[:, SUBLANES - (p["f_conv_w"].shape[1] - 1):])
        x = _matmul(u.reshape(m, d_ff), p["f_w_down_lp"], layer=l, bm=1024 if big else 128, bn=256,
                    bk=d_ff, epilogue="resid", res=x.reshape(m, d),
                    gate=gate_rows(gate)).reshape(bsz, t, d)
        if l == n_a - 1:
            hkv = _norm(x, p["kv_norm_w"].reshape(1, 1, d), plus_one=False, out_dtype=MXU_DTYPE)
            hkv = hkv.reshape(m, d)
            k_new = _matmul(hkv, p["w_kv"], n=d, bm=2048 if big else 128, bn=512).reshape(bsz, t, d)
            v_new = _matmul(hkv, p["w_kv"], n_off=d, n=d, bm=2048 if big else 128, bn=512)
            v_new = v_new.reshape(bsz, t, d)
            w_fg = jnp.pad(p["w_fgate"], ((0, 0), (0, LANES - nh_fox)))
            b_fg = jnp.pad(p["b_fgate"].astype(F32), (0, LANES - nh_fox)).reshape(1, LANES)
            lf_pad = _matmul(hkv, w_fg, bm=2048 if big else 128, bn=LANES, epilogue="logsig", bias=b_fg)
            lf_pad = lf_pad.reshape(bsz, t, LANES)
            lf_new = lf_pad[:, :, :nh_fox]
            tb = 256
            lf_all = lf_pad
            if past > 0:
                lf_p = jnp.pad(lf_past.astype(F32), ((0, 0), (0, 0), (0, LANES - nh_fox)))
                lf_all = jnp.concatenate([lf_p, lf_pad], axis=1)
            tot = lf_all.shape[1]
            lf_all = jnp.pad(lf_all, ((0, 0), (0, (-tot) % tb), (0, 0)))
            c_row, c_t = _cumsum_time(lf_all, tb)
    y_out = _norm(x, p["final_norm_w"].reshape(1, 1, d), plus_one=False, out_dtype=x.dtype)
    return (y_out, jnp.stack(new_ssm).astype(x.dtype), jnp.stack(new_conv), jnp.stack(new_ffn),
            k_new.reshape(bsz, t, nh_fox, dh), v_new.reshape(bsz, t, nh_fox, dh), lf_new.astype(x.dtype))


def kernel(x_prompt, x_sample, c_prompt, c_sample, cache_k, cache_v, cache_logf, state_ssm, state_conv, state_ffn_conv, w_ada, b_ada, m_w_in, m_conv_w, m_conv_b, m_dt_bias, m_a_log, m_d, m_norm_w, m_w_out, kv_norm_w, w_kv, w_fgate, b_fgate, w_q, w_o, f_w_up, f_conv_w, f_conv_b, f_w_down, final_norm_w):
    p = dict(w_ada=w_ada, b_ada=b_ada, m_w_in=m_w_in, m_conv_w=m_conv_w, m_conv_b=m_conv_b,
             m_dt_bias=m_dt_bias, m_a_log=m_a_log, m_d=m_d, m_norm_w=m_norm_w, m_w_out=m_w_out,
             kv_norm_w=kv_norm_w, w_kv=w_kv, w_fgate=w_fgate, b_fgate=b_fgate, w_q=w_q, w_o=w_o,
             f_w_up=f_w_up, f_conv_w=f_conv_w, f_conv_b=f_conv_b, f_w_down=f_w_down,
             final_norm_w=final_norm_w)
    p["f_w_down_lp"] = f_w_down.astype(MXU_DTYPE)
    bp, d = c_prompt.shape
    bs = c_sample.shape[0]
    n_a, depth = m_w_in.shape[0], w_ada.shape[0]
    nheads, nstate = state_ssm.shape[2], state_ssm.shape[4]
    hd = state_ssm.shape[3]
    dtp = x_prompt.dtype

    rows = bp + bs
    c_rows = jnp.pad(jnp.concatenate([c_prompt, c_sample], axis=0), ((0, (-rows) % 16), (0, 0)))
    mod = _ada_all(c_rows, w_ada, b_ada)

    def mods_for(lo, hi):
        return [tuple(mod[s, lo:hi, i * d:(i + 1) * d][:, None, :] for i in range(3))
                for s in range(2 * depth)]

    ssm0 = jnp.zeros((n_a, bp, nheads, hd, nstate), F32)
    conv0 = jnp.zeros((n_a, bp, m_conv_w.shape[1] - 1, m_conv_w.shape[2]), dtp)
    ffn0 = jnp.zeros((depth, bp, f_conv_w.shape[1] - 1, f_conv_w.shape[2]), dtp)
    kv0 = jnp.zeros((bp, 0) + cache_k.shape[2:], dtp)
    lf0 = jnp.zeros((bp, 0, cache_logf.shape[2]), F32)
    out_p = _trunk(x_prompt, mods_for(0, bp), ssm0, conv0, ffn0, kv0, kv0, lf0, p)
    out_s = _trunk(x_sample, mods_for(bp, rows), state_ssm, state_conv, state_ffn_conv,
                   cache_k, cache_v, cache_logf, p)
    return (out_p[0], out_s[0]) + out_p[1:] + out_s[1:]
```

```python
import functools
import math

import jax
import jax.numpy as jnp
from jax import lax
from jax.experimental import pallas as pl
from jax.experimental.pallas import tpu as pltpu

F32 = jnp.float32
MXU_DTYPE = jnp.bfloat16
EPS = 1e-6
LANES = 128
SUBLANES = 8
PACKED_ROWS = 16
VMEM_LIMIT_BYTES = 56 * 1024 * 1024
SSD_CHUNK = 64
NEG_INF = float("-inf")
LOG2E = math.log2(math.e)
MM_ROWS = 2080
MM_ROWS_WIDE_K = 1040
NORM_ROWS = 320
SEQ_TABLE_ROWS = 16


def _tile(dim, pref, align):
    t = (min(pref, dim) // align) * align
    while t >= align:
        if dim % t == 0:
            return t
        t -= align
    return dim


def _params(*sem):
    return pltpu.CompilerParams(dimension_semantics=sem, vmem_limit_bytes=VMEM_LIMIT_BYTES)


def _silu(x):
    return x * jax.nn.sigmoid(x)


def _softplus(x):
    return jnp.maximum(x, 0.0) + jnp.log1p(jnp.exp(-jnp.abs(x)))


def _split3(a):
    a1 = a.astype(MXU_DTYPE)
    r1 = a - a1.astype(F32)
    a2 = r1.astype(MXU_DTYPE)
    r2 = r1 - a2.astype(F32)
    return a1, a2, r2.astype(MXU_DTYPE)


def _dot01(m01, a, dims=(((1,), (0,)), ((), ()))):
    m = m01.astype(MXU_DTYPE)
    out = None
    for t in _split3(a):
        part = lax.dot_general(m, t, dims, preferred_element_type=F32)
        out = part if out is None else out + part
    return out


def _tile_segments(rows_per_tile, ntiles, bounds):
    out = []
    for i in range(ntiles):
        lo, hi = i * rows_per_tile, (i + 1) * rows_per_tile
        out.append(tuple((max(s, lo) - lo, min(e, hi) - lo, q) for s, e, q in bounds
                         if max(s, lo) < min(e, hi)))
    return tuple(out)


def _for_tile_segments(i, tile_segs, emit):
    groups = {}
    for t, segs in enumerate(tile_segs):
        groups.setdefault(segs, []).append(t)
    if len(groups) == 1:
        emit(tile_segs[0])
        return
    for segs, tiles in groups.items():
        runs = []
        for t in tiles:
            if runs and runs[-1][1] == t - 1:
                runs[-1][1] = t
            else:
                runs.append([t, t])
        cond = None
        for a, b in runs:
            c = (i == a) if a == b else ((i >= a) & (i <= b))
            cond = c if cond is None else (cond | c)
        pl.when(cond)(functools.partial(emit, segs))


def _ada_kernel(c_ref, w_ref, b_ref, o_ref):
    a = _silu(c_ref[...]).astype(MXU_DTYPE)
    w = w_ref[...].astype(MXU_DTYPE)
    o_ref[...] = jnp.dot(a, w, preferred_element_type=F32) + b_ref[...]


def _ada_all(c_rows, w_ada, b_ada):
    r, d = c_rows.shape
    s = w_ada.shape[0] * w_ada.shape[1]
    n = w_ada.shape[-1]
    w = w_ada.reshape(s, d, n)
    b = b_ada.reshape(s, 1, n)
    tn = _tile(n, 512, LANES)
    return pl.pallas_call(
        _ada_kernel,
        out_shape=jax.ShapeDtypeStruct((s, r, n), F32),
        grid=(s, n // tn),
        in_specs=[
            pl.BlockSpec((r, d), lambda i, j: (0, 0)),
            pl.BlockSpec((None, d, tn), lambda i, j: (i, 0, j)),
            pl.BlockSpec((None, 1, tn), lambda i, j: (i, 0, j)),
        ],
        out_specs=pl.BlockSpec((None, r, tn), lambda i, j: (i, 0, j)),
        compiler_params=_params("parallel", "parallel"),
        name="ada_mod",
    )(c_rows, w, b)


def _rms(x):
    return x * lax.rsqrt(jnp.mean(x * x, axis=-1, keepdims=True) + EPS)


def _norm_mod_kernel(x_ref, scale_ref, shift_ref, o_ref, *, tile_segs):
    y = _rms(x_ref[...])

    def emit(segs):
        for lo, hi, s in segs:
            v = y[lo:hi, :] * (1.0 + scale_ref[s:s + 1, :]) + shift_ref[s:s + 1, :]
            o_ref[lo:hi, :] = v.astype(o_ref.dtype)

    _for_tile_segments(pl.program_id(0), tile_segs, emit)


def _norm_mod(x, scale_tbl, shift_tbl, bounds):
    r, d = x.shape
    tt = _tile(r, NORM_ROWS, PACKED_ROWS)
    nt = r // tt
    s = scale_tbl.shape[0]
    return pl.pallas_call(
        functools.partial(_norm_mod_kernel, tile_segs=_tile_segments(tt, nt, bounds)),
        out_shape=jax.ShapeDtypeStruct((r, d), MXU_DTYPE),
        grid=(nt,),
        in_specs=[pl.BlockSpec((tt, d), lambda i: (i, 0)),
                  pl.BlockSpec((s, d), lambda i: (0, 0)),
                  pl.BlockSpec((s, d), lambda i: (0, 0))],
        out_specs=pl.BlockSpec((tt, d), lambda i: (i, 0)),
        compiler_params=_params("parallel"),
        name="rmsnorm_mod",
    )(x, scale_tbl, shift_tbl)


def _norm_w_kernel(x_ref, w_ref, o_ref):
    o_ref[...] = (_rms(x_ref[...]) * w_ref[...]).astype(o_ref.dtype)


def _norm_w(xv, w, *, boff, nb, t, out_dtype):
    d = xv.shape[-1]
    tt = _tile(t, NORM_ROWS, PACKED_ROWS)
    return pl.pallas_call(
        _norm_w_kernel,
        out_shape=jax.ShapeDtypeStruct((nb, t, d), out_dtype),
        grid=(nb, t // tt),
        in_specs=[pl.BlockSpec((None, tt, d), lambda i, j: (i + boff, j, 0)),
                  pl.BlockSpec((1, d), lambda i, j: (0, 0))],
        out_specs=pl.BlockSpec((None, tt, d), lambda i, j: (i, j, 0)),
        compiler_params=_params("parallel", "parallel"),
        name="rmsnorm_w",
    )(xv, w.astype(F32).reshape(1, d))


def _mm_kernel(*refs, nk, epilogue, out_scale, tile_segs):
    x_ref, w_ref = refs[0], refs[1]
    pos = 2
    if epilogue == "resid":
        res_ref, gate_ref = refs[2], refs[3]
        pos = 4
    elif epilogue == "logsig":
        bias_ref = refs[2]
        pos = 3
    o_ref = refs[pos]
    part = jnp.dot(x_ref[...], w_ref[...].astype(MXU_DTYPE), preferred_element_type=F32)

    def finish(acc):
        if out_scale is not None:
            acc = acc * out_scale
        if epilogue == "resid":
            def emit(segs):
                for lo, hi, s in segs:
                    v = res_ref[lo:hi, :] + gate_ref[s:s + 1, :] * acc[lo:hi, :]
                    o_ref[lo:hi, :] = v.astype(o_ref.dtype)

            _for_tile_segments(pl.program_id(0), tile_segs, emit)
            return
        if epilogue == "logsig":
            acc = -_softplus(-(acc + bias_ref[...]))
        o_ref[...] = acc.astype(o_ref.dtype)

    if nk == 1:
        finish(part)
    else:
        acc_ref = refs[pos + 1]
        k = pl.program_id(2)

        @pl.when(k == 0)
        def _():
            acc_ref[...] = part

        @pl.when(k > 0)
        def _():
            acc_ref[...] += part

        @pl.when(k == nk - 1)
        def _():
            finish(acc_ref[...])


def _matmul(x, w, *, layer=0, n_off=0, n=None, out_dtype=F32, bm=None, bn=512, bk=4096,
            epilogue=None, res=None, gate_tbl=None, bounds=None, bias=None, out_scale=None):
    m, kdim = x.shape
    if w.ndim == 2:
        w = w[None]
    n = w.shape[2] if n is None else n
    bm = _tile(m, MM_ROWS if bm is None else bm, PACKED_ROWS)
    bn = _tile(math.gcd(n, n_off), bn, LANES)
    bk = _tile(kdim, bk, LANES)
    joff = n_off // bn
    nk = kdim // bk
    ins = [x, w]
    x_mode = dict(pipeline_mode=pl.Buffered(1)) if nk == 1 and n // bn > 1 else {}
    specs = [pl.BlockSpec((bm, bk), lambda i, j, k: (i, k), **x_mode),
             pl.BlockSpec((None, bk, bn), lambda i, j, k: (layer, k, j + joff))]
    tile_segs = None
    if epilogue == "resid":
        ins += [res, gate_tbl]
        specs.append(pl.BlockSpec((bm, bn), lambda i, j, k: (i, j)))
        specs.append(pl.BlockSpec((gate_tbl.shape[0], bn), lambda i, j, k: (0, j)))
        tile_segs = _tile_segments(bm, m // bm, bounds)
    elif epilogue == "logsig":
        ins.append(bias)
        specs.append(pl.BlockSpec((1, bn), lambda i, j, k: (0, j)))
    scratch = [pltpu.VMEM((bm, bn), F32)] if nk > 1 else []
    return pl.pallas_call(
        functools.partial(_mm_kernel, nk=nk, epilogue=epilogue, out_scale=out_scale,
                          tile_segs=tile_segs),
        out_shape=jax.ShapeDtypeStruct((m, n), out_dtype),
        grid=(m // bm, n // bn, nk),
        in_specs=specs,
        out_specs=pl.BlockSpec((bm, bn), lambda i, j, k: (i, j)),
        scratch_shapes=scratch,
        compiler_params=_params("parallel", "parallel", "arbitrary"),
        name="matmul_" + (epilogue or "plain"),
    )(*ins)


def _conv_kernel(*refs, width, tt, gated):
    if gated:
        a_ref, x_ref, buf_ref, w_ref, b_ref, y_ref, tail_ref, xx_ref = refs
    else:
        x_ref, buf_ref, w_ref, b_ref, y_ref, tail_ref, xx_ref = refs
    ti = pl.program_id(2)

    @pl.when(ti == 0)
    def _():
        xx_ref[0:SUBLANES, :] = buf_ref[...]

    xx_ref[SUBLANES:SUBLANES + tt, :] = x_ref[...]
    acc = b_ref[...]
    for k in range(width):
        lo = SUBLANES - (width - 1) + k
        acc = acc + w_ref[k:k + 1, :] * xx_ref[lo:lo + tt, :]
    y = _silu(acc)
    if gated:
        y = y * a_ref[...]
    y_ref[...] = y.astype(y_ref.dtype)
    tail = xx_ref[tt:tt + SUBLANES, :]
    xx_ref[0:SUBLANES, :] = tail
    tail_ref[...] = tail


def _causal_conv(srcv, buf, w, b, *, boff, t, col_off, cdim, tc, tt, out_dtype, gate_col_off=None):
    nb = buf.shape[0]
    width = w.shape[0]
    assert t >= SUBLANES and width - 1 <= SUBLANES
    tt = _tile(t, tt, SUBLANES)
    tc = _tile(math.gcd(cdim, col_off, gate_col_off or 0), tc, LANES)
    coff = col_off // tc
    buf8 = jnp.pad(buf.astype(F32), ((0, 0), (SUBLANES - (width - 1), 0), (0, 0)))
    w8 = jnp.pad(w, ((0, SUBLANES - width), (0, 0)))
    b2 = b.reshape(1, cdim)
    gated = gate_col_off is not None
    ins, specs = [], []
    if gated:
        goff = gate_col_off // tc
        ins.append(srcv)
        specs.append(pl.BlockSpec((None, tt, tc), lambda i, j, k: (i + boff, k, j + goff)))
    ins += [srcv, buf8, w8, b2]
    specs += [
        pl.BlockSpec((None, tt, tc), lambda i, j, k: (i + boff, k, j + coff)),
        pl.BlockSpec((None, SUBLANES, tc), lambda i, j, k: (i, 0, j)),
        pl.BlockSpec((SUBLANES, tc), lambda i, j, k: (0, j)),
        pl.BlockSpec((1, tc), lambda i, j, k: (0, j)),
    ]
    return pl.pallas_call(
        functools.partial(_conv_kernel, width=width, tt=tt, gated=gated),
        out_shape=(jax.ShapeDtypeStruct((nb, t, cdim), out_dtype),
                   jax.ShapeDtypeStruct((nb, SUBLANES, cdim), F32)),
        grid=(nb, cdim // tc, t // tt),
        in_specs=specs,
        out_specs=(pl.BlockSpec((None, tt, tc), lambda i, j, k: (i, k, j)),
                   pl.BlockSpec((None, SUBLANES, tc), lambda i, j, k: (i, 0, j))),
        scratch_shapes=[pltpu.VMEM((tt + SUBLANES, tc), F32)],
        compiler_params=_params("parallel", "parallel", "arbitrary"),
        name="causal_conv_gated" if gated else "causal_conv",
    )(*ins)


def _ssd_kernel(x_ref, b_ref, c_ref, z_ref, dt_ref, dtb_ref, alog_ref, dsk_ref, nw_ref, h0_ref,
                y_ref, hout_ref, ht_ref, yg_ref, *, hpg, hd, ln, nheads):
    g = pl.program_id(1)
    c = pl.program_id(2)
    gw = hpg * hd
    per_blk = LANES // hd
    nblk = gw // LANES

    @pl.when(c == 0)
    def _():
        ht_ref[...] = h0_ref[...].T

    dt = _softplus(dt_ref[...] + dtb_ref[...])
    a_neg = -jnp.exp(alog_ref[...])
    shift = lax.rem(nheads - g * hpg, nheads)
    dt_g = pltpu.roll(dt, shift, axis=1)
    da_g = pltpu.roll(dt * a_neg, shift, axis=1)
    dsk_g = pltpu.roll(jnp.broadcast_to(dsk_ref[...], (SUBLANES, nheads)), shift, axis=1)

    row = lax.broadcasted_iota(jnp.int32, (ln, ln), 0)
    col = lax.broadcasted_iota(jnp.int32, (ln, ln), 1)
    tril = row >= col
    a_cum = _dot01(tril, da_g)
    hp = max(hpg, SUBLANES)
    sel = (lax.broadcasted_iota(jnp.int32, (hp, nheads), 0)
           == lax.broadcasted_iota(jnp.int32, (hp, nheads), 1))
    nt = (((1,), (1,)), ((), ()))
    a_cum_t = _dot01(sel, a_cum, nt)

    bmat = b_ref[...].astype(MXU_DTYPE)
    cmat = c_ref[...].astype(MXU_DTYPE)
    cb = lax.dot_general(cmat, bmat, nt, preferred_element_type=F32)

    lane = lax.broadcasted_iota(jnp.int32, (1, LANES), 1)
    ssq = jnp.zeros((ln, 1), F32)
    for j in range(nblk):
        lo, hi = j * LANES, (j + 1) * LANES
        x_blk = x_ref[:, lo:hi]
        acol = dcol = dsk_row = None
        for w in range(per_blk):
            i = j * per_blk + w
            a_b = jnp.broadcast_to(a_cum[:, i:i + 1], (ln, LANES))
            d_b = jnp.broadcast_to(dt_g[:, i:i + 1], (ln, LANES))
            s_b = jnp.broadcast_to(dsk_g[0:1, i:i + 1], (1, LANES))
            if w == 0:
                acol, dcol, dsk_row = a_b, d_b, s_b
            else:
                in_w = lane >= w * hd
                acol = jnp.where(in_w, a_b, acol)
                dcol = jnp.where(in_w, d_b, dcol)
                dsk_row = jnp.where(in_w, s_b, dsk_row)
        alast = acol[ln - 1:ln, :]
        xdt = x_blk * dcol
        xdt_m = xdt.astype(MXU_DTYPE)
        xw_m = (xdt * jnp.exp(alast - acol)).astype(MXU_DTYPE)
        ydiag = None
        for w in range(per_blk):
            i = j * per_blk + w
            seg = acol[:, w * hd:w * hd + 1] - a_cum_t[i:i + 1, :]
            dec = jnp.exp(jnp.where(tril, seg, NEG_INF))
            yw = jnp.dot((cb * dec).astype(MXU_DTYPE), xdt_m, preferred_element_type=F32)
            ydiag = yw if w == 0 else jnp.where(lane >= w * hd, yw, ydiag)
        h_blk = ht_ref[:, lo:hi]
        yoff = jnp.dot(cmat, h_blk.astype(MXU_DTYPE), preferred_element_type=F32) * jnp.exp(acol)
        st = lax.dot_general(bmat, xw_m, (((0,), (0,)), ((), ())), preferred_element_type=F32)
        ht_ref[:, lo:hi] = h_blk * jnp.exp(alast) + st
        y = ydiag + yoff + dsk_row * x_blk
        y = y * _silu(z_ref[:, lo:hi])
        yg_ref[:, lo:hi] = y
        ssq = ssq + jnp.sum(y * y, axis=-1, keepdims=True)

    inv = lax.rsqrt(ssq * (1.0 / gw) + EPS)
    y_ref[...] = (yg_ref[...] * inv * nw_ref[...]).astype(y_ref.dtype)

    @pl.when(c == pl.num_programs(2) - 1)
    def _():
        hout_ref[...] = ht_ref[...].T


def _ssd(xbc, zxv, dtv, dt_bias, a_log, d_skip, norm_w, h0, *, boff, d_inner, ngroups, nstate, chunk):
    nb, t, _ = xbc.shape
    nheads = dtv.shape[-1]
    hd = d_inner // nheads
    hpg = nheads // ngroups
    gw = hpg * hd
    assert nheads <= LANES
    assert LANES % hd == 0 and gw % LANES == 0 and nstate % LANES == 0 and t % chunk == 0
    nc = t // chunk
    h0g = h0.astype(F32).reshape(nb, ngroups, gw, nstate)
    vec = lambda a: a.astype(F32).reshape(1, nheads)
    xoff = d_inner // nstate
    kern = functools.partial(_ssd_kernel, hpg=hpg, hd=hd, ln=chunk, nheads=nheads)
    y, hout = pl.pallas_call(
        kern,
        out_shape=(jax.ShapeDtypeStruct((nb, t, d_inner), MXU_DTYPE),
                   jax.ShapeDtypeStruct((nb, ngroups, gw, nstate), F32)),
        grid=(nb, ngroups, nc),
        in_specs=[
            pl.BlockSpec((None, chunk, gw), lambda b, g, c: (b, c, g)),
            pl.BlockSpec((None, chunk, nstate), lambda b, g, c: (b, c, xoff + g)),
            pl.BlockSpec((None, chunk, nstate), lambda b, g, c: (b, c, xoff + ngroups + g)),
            pl.BlockSpec((None, chunk, gw), lambda b, g, c: (b + boff, c, g)),
            pl.BlockSpec((None, chunk, nheads), lambda b, g, c: (b + boff, c, 0)),
            pl.BlockSpec((1, nheads), lambda b, g, c: (0, 0)),
            pl.BlockSpec((1, nheads), lambda b, g, c: (0, 0)),
            pl.BlockSpec((1, nheads), lambda b, g, c: (0, 0)),
            pl.BlockSpec((1, gw), lambda b, g, c: (0, g)),
            pl.BlockSpec((None, None, gw, nstate), lambda b, g, c: (b, g, 0, 0)),
        ],
        out_specs=(pl.BlockSpec((None, chunk, gw), lambda b, g, c: (b, c, g)),
                   pl.BlockSpec((None, None, gw, nstate), lambda b, g, c: (b, g, 0, 0))),
        scratch_shapes=[pltpu.VMEM((nstate, gw), F32), pltpu.VMEM((chunk, gw), F32)],
        compiler_params=_params("parallel", "parallel", "arbitrary"),
        name="ssd_scan",
    )(xbc, xbc, xbc, zxv, dtv, vec(dt_bias), vec(a_log), vec(d_skip),
      norm_w.astype(F32).reshape(1, d_inner), h0g)
    return y, hout.reshape(nb, nheads, hd, nstate)


def _cumsum_kernel(lf_ref, c_ref, ct_ref, carry_ref, *, tb):
    @pl.when(pl.program_id(1) == 0)
    def _():
        carry_ref[...] = jnp.zeros_like(carry_ref)

    row = lax.broadcasted_iota(jnp.int32, (tb, tb), 0)
    col = lax.broadcasted_iota(jnp.int32, (tb, tb), 1)
    cs = _dot01(row >= col, lf_ref[...]) + carry_ref[0:1, :]
    c_ref[...] = cs
    ct_ref[...] = cs.T
    carry_ref[...] = jnp.broadcast_to(cs[tb - 1:tb, :], carry_ref.shape)


def _cumsum_time(lfv, *, t, tb):
    bsz, _, w = lfv.shape
    return pl.pallas_call(
        functools.partial(_cumsum_kernel, tb=tb),
        out_shape=(jax.ShapeDtypeStruct((bsz, t, w), F32), jax.ShapeDtypeStruct((bsz, w, t), F32)),
        grid=(bsz, t // tb),
        in_specs=[pl.BlockSpec((None, tb, w), lambda b, i: (b, i, 0))],
        out_specs=(pl.BlockSpec((None, tb, w), lambda b, i: (b, i, 0)),
                   pl.BlockSpec((None, w, tb), lambda b, i: (b, 0, i))),
        scratch_shapes=[pltpu.VMEM((SUBLANES, w), F32)],
        compiler_params=_params("parallel", "arbitrary"),
        name="logf_cumsum",
    )(lfv)


def _head_column(c_blk, h, width):
    rolled = pltpu.roll(c_blk, lax.rem(LANES - h, LANES), axis=1)
    return jnp.broadcast_to(rolled[:, 0:1], (c_blk.shape[0], width))


def _aug_cols(c_blk, h, for_query):
    col = _head_column(c_blk, h, LANES) * LOG2E
    c1, c2, c3 = (t.astype(F32) for t in _split3(col))
    one = jnp.ones_like(col)
    terms = (c1, c2, c3, one, one, one) if for_query else (one, one, one, -c1, -c2, -c3)
    lane = lax.broadcasted_iota(jnp.int32, col.shape, 1)
    out = jnp.zeros_like(col)
    for idx, term in enumerate(terms):
        out = jnp.where(lane == idx, term, out)
    return out.astype(MXU_DTYPE)


def _fox_prefill_kernel(q_ref, k_ref, v_ref, cq_ref, ck_ref, o_ref, m_ref, l_ref, acc_ref, qa_ref, *, tq):
    h = pl.program_id(1)
    qi = pl.program_id(2)
    ki = pl.program_id(3)
    dh = q_ref.shape[-1]

    @pl.when(ki == 0)
    def _():
        m_ref[...] = jnp.full_like(m_ref, NEG_INF)
        l_ref[...] = jnp.zeros_like(l_ref)
        acc_ref[...] = jnp.zeros_like(acc_ref)
        qa_ref[:, 0:dh] = q_ref[...]
        qa_ref[:, dh:dh + LANES] = _aug_cols(cq_ref[...], h, True)

    def step(diagonal):
        ka = jnp.concatenate([k_ref[...].astype(MXU_DTYPE), _aug_cols(ck_ref[...], h, False)], axis=1)
        s = lax.dot_general(qa_ref[...], ka, (((1,), (1,)), ((), ())), preferred_element_type=F32)
        if diagonal:
            row = lax.broadcasted_iota(jnp.int32, (tq, tq), 0)
            col = lax.broadcasted_iota(jnp.int32, (tq, tq), 1)
            s = jnp.where(col <= row, s, NEG_INF)
        m_prev = m_ref[...]
        m_new = jnp.maximum(m_prev, jnp.max(s, axis=-1, keepdims=True))
        p = jnp.exp2(s - m_new)
        alpha = jnp.exp2(m_prev - m_new)
        l_ref[...] = alpha * l_ref[...] + jnp.sum(p, axis=-1, keepdims=True)
        acc_ref[...] = alpha * acc_ref[...] + jnp.dot(
            p.astype(MXU_DTYPE), v_ref[...].astype(MXU_DTYPE), preferred_element_type=F32)
        m_ref[...] = m_new

    @pl.when(ki < qi)
    def _():
        step(False)

    @pl.when(ki == qi)
    def _():
        step(True)

    @pl.when(ki == pl.num_programs(3) - 1)
    def _():
        o_ref[...] = (acc_ref[...] / l_ref[...]).astype(o_ref.dtype)


def _fox_prefill(qv, kv, vv, c, *, t, dh):
    bsz, _, d = qv.shape
    nh = d // dh
    tq = _tile(t, 1024, PACKED_ROWS)
    nq = t // tq
    kclamp = lambda qi, ki: jnp.minimum(ki, qi)
    return pl.pallas_call(
        functools.partial(_fox_prefill_kernel, tq=tq),
        out_shape=jax.ShapeDtypeStruct((bsz, t, d), MXU_DTYPE),
        grid=(bsz, nh, nq, nq),
        in_specs=[
            pl.BlockSpec((None, tq, dh), lambda b, h, qi, ki: (b, qi, h)),
            pl.BlockSpec((None, tq, dh), lambda b, h, qi, ki: (b, kclamp(qi, ki), h)),
            pl.BlockSpec((None, tq, dh), lambda b, h, qi, ki: (b, kclamp(qi, ki), h)),
            pl.BlockSpec((None, tq, LANES), lambda b, h, qi, ki: (b, qi, 0)),
            pl.BlockSpec((None, tq, LANES), lambda b, h, qi, ki: (b, kclamp(qi, ki), 0)),
        ],
        out_specs=pl.BlockSpec((None, tq, dh), lambda b, h, qi, ki: (b, qi, h)),
        scratch_shapes=[pltpu.VMEM((tq, 1), F32), pltpu.VMEM((tq, 1), F32),
                        pltpu.VMEM((tq, dh), F32), pltpu.VMEM((tq, dh + LANES), MXU_DTYPE)],
        compiler_params=_params("parallel", "parallel", "parallel", "arbitrary"),
        name="fox_prefill",
    )(qv, kv, vv, c, c)


def _fox_decode_kernel(q_ref, kp_ref, vp_ref, kn_ref, vn_ref, cq_ref, ckt_ref, o_ref, *, past, tq):
    h = pl.program_id(1)
    nt = (((1,), (1,)), ((), ()))
    q = q_ref[...]
    ck = ckt_ref[pl.ds(h, 1), :]
    cq_p = _head_column(cq_ref[...], h, past)
    s_p = lax.dot_general(q, kp_ref[...].astype(MXU_DTYPE), nt, preferred_element_type=F32)
    s_p = s_p + (cq_p - ck[:, 0:past]) * LOG2E
    s_n = lax.dot_general(q, kn_ref[...].astype(MXU_DTYPE), nt, preferred_element_type=F32)
    s_n = s_n + (cq_p[:, 0:tq] - ck[:, past:past + tq]) * LOG2E
    row = lax.broadcasted_iota(jnp.int32, (tq, tq), 0)
    col = lax.broadcasted_iota(jnp.int32, (tq, tq), 1)
    s_n = jnp.where(col <= row, s_n, NEG_INF)
    m = jnp.maximum(jnp.max(s_p, axis=-1, keepdims=True), jnp.max(s_n, axis=-1, keepdims=True))
    p_p = jnp.exp2(s_p - m)
    p_n = jnp.exp2(s_n - m)
    l = jnp.sum(p_p, axis=-1, keepdims=True) + jnp.sum(p_n, axis=-1, keepdims=True)
    acc = jnp.dot(p_p.astype(MXU_DTYPE), vp_ref[...].astype(MXU_DTYPE), preferred_element_type=F32)
    acc = acc + jnp.dot(p_n.astype(MXU_DTYPE), vn_ref[...].astype(MXU_DTYPE), preferred_element_type=F32)
    o_ref[...] = (acc / l).astype(o_ref.dtype)


def _fox_decode(qv, k_past, v_past, knv, vnv, c, ct, *, boff, t, dh):
    nb, past, d = k_past.shape
    nh = d // dh
    tpad = ct.shape[-1]
    assert past % t == 0 and past % LANES == 0
    new_blk = lambda b, h: (b + boff, 0, h)
    return pl.pallas_call(
        functools.partial(_fox_decode_kernel, past=past, tq=t),
        out_shape=jax.ShapeDtypeStruct((nb, t, d), MXU_DTYPE),
        grid=(nb, nh),
        in_specs=[
            pl.BlockSpec((None, t, dh), new_blk),
            pl.BlockSpec((None, past, dh), lambda b, h: (b, 0, h)),
            pl.BlockSpec((None, past, dh), lambda b, h: (b, 0, h)),
            pl.BlockSpec((None, t, dh), new_blk),
            pl.BlockSpec((None, t, dh), new_blk),
            pl.BlockSpec((None, t, LANES), lambda b, h: (b, past // t, 0)),
            pl.BlockSpec((None, LANES, tpad), lambda b, h: (b, 0, 0)),
        ],
        out_specs=pl.BlockSpec((None, t, dh), lambda b, h: (b, 0, h)),
        compiler_params=_params("parallel", "parallel"),
        name="fox_decode",
    )(qv, k_past, v_past, knv, vnv, c, ct)


def _forward(x, mod, passes, bounds, k_past, v_past, lf_past, p):
    r, d = x.shape
    depth = p["w_ada"].shape[0]
    n_a = p["m_w_in"].shape[0]
    d_inner = p["m_w_out"].shape[1]
    conv_dim = p["m_conv_w"].shape[2]
    nheads = p["m_dt_bias"].shape[1]
    nstate = passes[0]["ssm0"].shape[-1]
    ngroups = (conv_dim - d_inner) // (2 * nstate)
    d_ff = p["f_w_down"].shape[1]
    nh_fox, dh = k_past.shape[2], k_past.shape[3]
    past = k_past.shape[1]

    def tables(s):
        return tuple(mod[s, :, i * d:(i + 1) * d] for i in range(3))

    def stack_rows(parts):
        return jnp.concatenate([a.reshape(-1, a.shape[-1]) for a in parts], axis=0)

    outs = [dict(ssm=[], conv=[], ffn=[]) for _ in passes]
    k_all = v_all = lf_pad = None
    cums = [None] * len(passes)
    for l in range(depth):
        shift, scale, gate = tables(2 * l)
        h = _norm_mod(x, scale, shift, bounds)
        if l < n_a:
            w_in = p["m_w_in"]
            zx = _matmul(h, w_in, layer=l, n=d_inner + conv_dim, bn=512)
            dt_raw = _matmul(h, w_in, layer=l, n_off=d_inner + conv_dim, n=nheads, bn=nheads)
            ys = []
            for ps, out in zip(passes, outs):
                zxv, dtv = ps["view"](zx), ps["view"](dt_raw)
                xbc, tail = _causal_conv(zxv, ps["conv0"][l], p["m_conv_w"][l], p["m_conv_b"][l],
                                         boff=ps["boff"], t=ps["t"], col_off=d_inner, cdim=conv_dim,
                                         tc=2048, tt=256, out_dtype=F32)
                out["conv"].append(tail[:, SUBLANES - (p["m_conv_w"].shape[1] - 1):])
                y, sst = _ssd(xbc, zxv, dtv, p["m_dt_bias"][l], p["m_a_log"][l], p["m_d"][l],
                              p["m_norm_w"][l], ps["ssm0"][l], boff=ps["boff"], d_inner=d_inner,
                              ngroups=ngroups, nstate=nstate, chunk=min(SSD_CHUNK, ps["t"]))
                out["ssm"].append(sst)
                ys.append(y)
            x = _matmul(stack_rows(ys), p["m_w_out"], layer=l, bm=MM_ROWS_WIDE_K, bn=256, bk=d_inner,
                        epilogue="resid", res=x, gate_tbl=gate, bounds=bounds)
        else:
            j = l - n_a
            q = _matmul(h, p["w_q"], layer=j, out_dtype=MXU_DTYPE, bn=512, out_scale=dh ** -0.5 * LOG2E)
            os_ = []
            for ps, cum in zip(passes, cums):
                if past == 0 or ps["prefill"]:
                    o = _fox_prefill(ps["view"](q), ps["view"](k_all), ps["view"](v_all), cum[0],
                                     t=ps["t"], dh=dh)
                else:
                    o = _fox_decode(ps["view"](q), ps["k_past"], ps["v_past"], ps["view"](k_all),
                                    ps["view"](v_all), cum[0], cum[1], boff=ps["boff"], t=ps["t"], dh=dh)
                os_.append(o)
            x = _matmul(stack_rows(os_), p["w_o"], layer=j, bm=MM_ROWS_WIDE_K, bn=512,
                        epilogue="resid", res=x, gate_tbl=gate, bounds=bounds)
        shift, scale, gate = tables(2 * l + 1)
        h = _norm_mod(x, scale, shift, bounds)
        up = _matmul(h, p["f_w_up"], layer=l, bn=512)
        us = []
        for ps, out in zip(passes, outs):
            u, tail = _causal_conv(ps["view"](up), ps["ffn0"][l], p["f_conv_w"][l], p["f_conv_b"][l],
                                   boff=ps["boff"], t=ps["t"], col_off=d_ff, cdim=d_ff, tc=d_ff, tt=64,
                                   out_dtype=MXU_DTYPE, gate_col_off=0)
            out["ffn"].append(tail[:, SUBLANES - (p["f_conv_w"].shape[1] - 1):])
            us.append(u)
        x = _matmul(stack_rows(us), p["f_w_down_lp"], layer=l, bm=MM_ROWS_WIDE_K, bn=256, bk=d_ff,
                    epilogue="resid", res=x, gate_tbl=gate, bounds=bounds)
        if l == n_a - 1:
            hkv = _norm_w(x[None], p["kv_norm_w"], boff=0, nb=1, t=r, out_dtype=MXU_DTYPE)[0]
            k_all = _matmul(hkv, p["w_kv"], n=d, bn=512)
            v_all = _matmul(hkv, p["w_kv"], n_off=d, n=d, bn=512)
            w_fg = jnp.pad(p["w_fgate"], ((0, 0), (0, LANES - nh_fox)))
            b_fg = jnp.pad(p["b_fgate"].astype(F32), (0, LANES - nh_fox)).reshape(1, LANES)
            lf_pad = _matmul(hkv, w_fg, bn=LANES, epilogue="logsig", bias=b_fg)
            tb = 256
            for i, ps in enumerate(passes):
                if ps["prefill"]:
                    assert ps["nb"] == 1 and ps["boff"] == 0 and ps["t"] % tb == 0
                    cums[i] = _cumsum_time(lf_pad[None], t=ps["t"], tb=tb)
                else:
                    lo = ps["boff"] * ps["t"]
                    lf_new = lf_pad[lo:lo + ps["nb"] * ps["t"]].reshape(ps["nb"], ps["t"], LANES)
                    lf_p = jnp.pad(ps["lf_past"].astype(F32), ((0, 0), (0, 0), (0, LANES - nh_fox)))
                    lf_all = jnp.concatenate([lf_p, lf_new], axis=1)
                    tot = lf_all.shape[1]
                    lf_all = jnp.pad(lf_all, ((0, 0), (0, (-tot) % tb), (0, 0)))
                    cums[i] = _cumsum_time(lf_all, t=lf_all.shape[1], tb=tb)
    results = []
    for ps, out in zip(passes, outs):
        lo, n = ps["boff"] * ps["t"], ps["nb"] * ps["t"]
        y_out = _norm_w(ps["view"](x), p["final_norm_w"], boff=ps["boff"], nb=ps["nb"], t=ps["t"],
                        out_dtype=x.dtype)
        seq = lambda a, w: a[lo:lo + n, :w].reshape((ps["nb"], ps["t"]) + ((nh_fox, dh) if w == d else (w,)))
        results.append((y_out, jnp.stack(out["ssm"]).astype(x.dtype), jnp.stack(out["conv"]),
                        jnp.stack(out["ffn"]), seq(k_all, d), seq(v_all, d),
                        seq(lf_pad, nh_fox).astype(x.dtype)))
    return results


def kernel(x_prompt, x_sample, c_prompt, c_sample, cache_k, cache_v, cache_logf, state_ssm, state_conv, state_ffn_conv, w_ada, b_ada, m_w_in, m_conv_w, m_conv_b, m_dt_bias, m_a_log, m_d, m_norm_w, m_w_out, kv_norm_w, w_kv, w_fgate, b_fgate, w_q, w_o, f_w_up, f_conv_w, f_conv_b, f_w_down, final_norm_w):
    p = dict(w_ada=w_ada, b_ada=b_ada, m_w_in=m_w_in, m_conv_w=m_conv_w, m_conv_b=m_conv_b,
             m_dt_bias=m_dt_bias, m_a_log=m_a_log, m_d=m_d, m_norm_w=m_norm_w, m_w_out=m_w_out,
             kv_norm_w=kv_norm_w, w_kv=w_kv, w_fgate=w_fgate, b_fgate=b_fgate, w_q=w_q, w_o=w_o,
             f_w_up=f_w_up, f_conv_w=f_conv_w, f_conv_b=f_conv_b, f_w_down=f_w_down,
             final_norm_w=final_norm_w)
    p["f_w_down_lp"] = f_w_down.astype(MXU_DTYPE)
    bp, tp, d = x_prompt.shape
    bs, ts, _ = x_sample.shape
    assert bp == 1 and tp % ts == 0 and ts % PACKED_ROWS == 0
    n_a, depth = m_w_in.shape[0], w_ada.shape[0]
    nheads, hd, nstate = state_ssm.shape[2], state_ssm.shape[3], state_ssm.shape[4]
    past = cache_k.shape[1]
    dtp = x_prompt.dtype
    r = tp + bs * ts

    nseq = bp + bs
    assert nseq <= SEQ_TABLE_ROWS
    c_rows = jnp.pad(jnp.concatenate([c_prompt, c_sample], axis=0), ((0, SEQ_TABLE_ROWS - nseq), (0, 0)))
    mod = _ada_all(c_rows, w_ada, b_ada)

    x = jnp.concatenate([x_prompt.reshape(tp, d), x_sample.reshape(bs * ts, d)], axis=0)
    bounds = ((0, tp, 0),) + tuple((tp + b * ts, tp + (b + 1) * ts, 1 + b) for b in range(bs))
    passes = [
        dict(prefill=True, boff=0, nb=1, t=tp, view=lambda a: a.reshape(1, r, a.shape[-1]),
             ssm0=jnp.zeros((n_a, bp, nheads, hd, nstate), F32),
             conv0=jnp.zeros((n_a, bp, m_conv_w.shape[1] - 1, m_conv_w.shape[2]), dtp),
             ffn0=jnp.zeros((depth, bp, f_conv_w.shape[1] - 1, f_conv_w.shape[2]), dtp)),
        dict(prefill=False, boff=tp // ts, nb=bs, t=ts, view=lambda a: a.reshape(r // ts, ts, a.shape[-1]),
             ssm0=state_ssm, conv0=state_conv, ffn0=state_ffn_conv, lf_past=cache_logf,
             k_past=cache_k.reshape(bs, past, d), v_past=cache_v.reshape(bs, past, d)),
    ]
    out_p, out_s = _forward(x, mod, passes, bounds, cache_k, cache_v, cache_logf, p)
    return (out_p[0], out_s[0]) + out_p[1:] + out_s[1:]
```

```python
import functools
import math

import jax
import jax.numpy as jnp
from jax import lax
from jax.experimental import pallas as pl
from jax.experimental.pallas import tpu as pltpu

F32 = jnp.float32
MXU_DTYPE = jnp.bfloat16
EPS = 1e-6
LANES = 128
SUBLANES = 8
PACKED_ROWS = 16
VMEM_LIMIT_BYTES = 56 * 1024 * 1024
SSD_CHUNK = 64
NEG_INF = float("-inf")
LOG2E = math.log2(math.e)
MM_ROWS = 2080
MM_ROWS_WIDE_K = 1040
NORM_ROWS = 320
SEQ_TABLE_ROWS = 16


def _tile(dim, pref, align):
    t = (min(pref, dim) // align) * align
    while t >= align:
        if dim % t == 0:
            return t
        t -= align
    return dim


def _params(*sem):
    return pltpu.CompilerParams(dimension_semantics=sem, vmem_limit_bytes=VMEM_LIMIT_BYTES)


def _silu(x):
    return x * jax.nn.sigmoid(x)


def _softplus(x):
    return jnp.maximum(x, 0.0) + jnp.log1p(jnp.exp(-jnp.abs(x)))


def _split3(a):
    a1 = a.astype(MXU_DTYPE)
    r1 = a - a1.astype(F32)
    a2 = r1.astype(MXU_DTYPE)
    r2 = r1 - a2.astype(F32)
    return a1, a2, r2.astype(MXU_DTYPE)


def _dot01(m01, a, dims=(((1,), (0,)), ((), ()))):
    m = m01.astype(MXU_DTYPE)
    out = None
    for t in _split3(a):
        part = lax.dot_general(m, t, dims, preferred_element_type=F32)
        out = part if out is None else out + part
    return out


def _tile_segments(rows_per_tile, ntiles, bounds):
    out = []
    for i in range(ntiles):
        lo, hi = i * rows_per_tile, (i + 1) * rows_per_tile
        out.append(tuple((max(s, lo) - lo, min(e, hi) - lo, q) for s, e, q in bounds
                         if max(s, lo) < min(e, hi)))
    return tuple(out)


def _for_tile_segments(i, tile_segs, emit):
    groups = {}
    for t, segs in enumerate(tile_segs):
        groups.setdefault(segs, []).append(t)
    if len(groups) == 1:
        emit(tile_segs[0])
        return
    for segs, tiles in groups.items():
        runs = []
        for t in tiles:
            if runs and runs[-1][1] == t - 1:
                runs[-1][1] = t
            else:
                runs.append([t, t])
        cond = None
        for a, b in runs:
            c = (i == a) if a == b else ((i >= a) & (i <= b))
            cond = c if cond is None else (cond | c)
        pl.when(cond)(functools.partial(emit, segs))


def _ada_kernel(c_ref, w_ref, b_ref, o_ref):
    a = _silu(c_ref[...]).astype(MXU_DTYPE)
    w = w_ref[...].astype(MXU_DTYPE)
    o_ref[...] = jnp.dot(a, w, preferred_element_type=F32) + b_ref[...]


def _ada_all(c_rows, w_ada, b_ada):
    r, d = c_rows.shape
    s = w_ada.shape[0] * w_ada.shape[1]
    n = w_ada.shape[-1]
    w = w_ada.reshape(s, d, n)
    b = b_ada.reshape(s, 1, n)
    tn = _tile(n, 512, LANES)
    return pl.pallas_call(
        _ada_kernel,
        out_shape=jax.ShapeDtypeStruct((s, r, n), F32),
        grid=(s, n // tn),
        in_specs=[
            pl.BlockSpec((r, d), lambda i, j: (0, 0)),
            pl.BlockSpec((None, d, tn), lambda i, j: (i, 0, j)),
            pl.BlockSpec((None, 1, tn), lambda i, j: (i, 0, j)),
        ],
        out_specs=pl.BlockSpec((None, r, tn), lambda i, j: (i, 0, j)),
        compiler_params=_params("parallel", "parallel"),
        name="ada_mod",
    )(c_rows, w, b)


def _rms(x):
    return x * lax.rsqrt(jnp.mean(x * x, axis=-1, keepdims=True) + EPS)


def _norm_mod_kernel(x_ref, scale_ref, shift_ref, o_ref, *, tile_segs):
    y = _rms(x_ref[...])

    def emit(segs):
        for lo, hi, s in segs:
            v = y[lo:hi, :] * (1.0 + scale_ref[s:s + 1, :]) + shift_ref[s:s + 1, :]
            o_ref[lo:hi, :] = v.astype(o_ref.dtype)

    _for_tile_segments(pl.program_id(0), tile_segs, emit)


def _norm_mod(x, scale_tbl, shift_tbl, bounds):
    r, d = x.shape
    tt = _tile(r, NORM_ROWS, PACKED_ROWS)
    nt = r // tt
    s = scale_tbl.shape[0]
    return pl.pallas_call(
        functools.partial(_norm_mod_kernel, tile_segs=_tile_segments(tt, nt, bounds)),
        out_shape=jax.ShapeDtypeStruct((r, d), MXU_DTYPE),
        grid=(nt,),
        in_specs=[pl.BlockSpec((tt, d), lambda i: (i, 0)),
                  pl.BlockSpec((s, d), lambda i: (0, 0)),
                  pl.BlockSpec((s, d), lambda i: (0, 0))],
        out_specs=pl.BlockSpec((tt, d), lambda i: (i, 0)),
        compiler_params=_params("parallel"),
        name="rmsnorm_mod",
    )(x, scale_tbl, shift_tbl)


def _norm_w_kernel(x_ref, w_ref, o_ref):
    o_ref[...] = (_rms(x_ref[...]) * w_ref[...]).astype(o_ref.dtype)


def _norm_w(xv, w, *, boff, nb, t, out_dtype):
    d = xv.shape[-1]
    tt = _tile(t, NORM_ROWS, PACKED_ROWS)
    return pl.pallas_call(
        _norm_w_kernel,
        out_shape=jax.ShapeDtypeStruct((nb, t, d), out_dtype),
        grid=(nb, t // tt),
        in_specs=[pl.BlockSpec((None, tt, d), lambda i, j: (i + boff, j, 0)),
                  pl.BlockSpec((1, d), lambda i, j: (0, 0))],
        out_specs=pl.BlockSpec((None, tt, d), lambda i, j: (i, j, 0)),
        compiler_params=_params("parallel", "parallel"),
        name="rmsnorm_w",
    )(xv, w.astype(F32).reshape(1, d))


def _mm_kernel(*refs, nk, epilogue, out_scale, tile_segs):
    x_ref, w_ref = refs[0], refs[1]
    pos = 2
    if epilogue == "resid":
        res_ref, gate_ref = refs[2], refs[3]
        pos = 4
    elif epilogue == "logsig":
        bias_ref = refs[2]
        pos = 3
    o_ref = refs[pos]
    part = jnp.dot(x_ref[...], w_ref[...].astype(MXU_DTYPE), preferred_element_type=F32)

    def finish(acc):
        if out_scale is not None:
            acc = acc * out_scale
        if epilogue == "resid":
            def emit(segs):
                for lo, hi, s in segs:
                    v = res_ref[lo:hi, :] + gate_ref[s:s + 1, :] * acc[lo:hi, :]
                    o_ref[lo:hi, :] = v.astype(o_ref.dtype)

            _for_tile_segments(pl.program_id(0), tile_segs, emit)
            return
        if epilogue == "logsig":
            acc = -_softplus(-(acc + bias_ref[...]))
        o_ref[...] = acc.astype(o_ref.dtype)

    if nk == 1:
        finish(part)
    else:
        acc_ref = refs[pos + 1]
        k = pl.program_id(2)

        @pl.when(k == 0)
        def _():
            acc_ref[...] = part

        @pl.when(k > 0)
        def _():
            acc_ref[...] += part

        @pl.when(k == nk - 1)
        def _():
            finish(acc_ref[...])


def _matmul(x, w, *, layer=0, n_off=0, n=None, out_dtype=F32, bm=None, bn=512, bk=4096,
            epilogue=None, res=None, gate_tbl=None, bounds=None, bias=None, out_scale=None):
    m, kdim = x.shape
    if w.ndim == 2:
        w = w[None]
    n = w.shape[2] if n is None else n
    bm = _tile(m, MM_ROWS if bm is None else bm, PACKED_ROWS)
    bn = _tile(math.gcd(n, n_off), bn, LANES)
    bk = _tile(kdim, bk, LANES)
    joff = n_off // bn
    nk = kdim // bk
    ins = [x, w]
    x_mode = dict(pipeline_mode=pl.Buffered(1)) if nk == 1 and n // bn > 1 else {}
    specs = [pl.BlockSpec((bm, bk), lambda i, j, k: (i, k), **x_mode),
             pl.BlockSpec((None, bk, bn), lambda i, j, k: (layer, k, j + joff))]
    tile_segs = None
    if epilogue == "resid":
        ins += [res, gate_tbl]
        specs.append(pl.BlockSpec((bm, bn), lambda i, j, k: (i, j)))
        specs.append(pl.BlockSpec((gate_tbl.shape[0], bn), lambda i, j, k: (0, j)))
        tile_segs = _tile_segments(bm, m // bm, bounds)
    elif epilogue == "logsig":
        ins.append(bias)
        specs.append(pl.BlockSpec((1, bn), lambda i, j, k: (0, j)))
    scratch = [pltpu.VMEM((bm, bn), F32)] if nk > 1 else []
    return pl.pallas_call(
        functools.partial(_mm_kernel, nk=nk, epilogue=epilogue, out_scale=out_scale,
                          tile_segs=tile_segs),
        out_shape=jax.ShapeDtypeStruct((m, n), out_dtype),
        grid=(m // bm, n // bn, nk),
        in_specs=specs,
        out_specs=pl.BlockSpec((bm, bn), lambda i, j, k: (i, j)),
        scratch_shapes=scratch,
        compiler_params=_params("parallel", "parallel", "arbitrary"),
        name="matmul_" + (epilogue or "plain"),
    )(*ins)


def _conv_kernel(*refs, width, tt, gated, nt, appending):
    refs = list(refs)
    a_ref = refs.pop(0) if gated else None
    x_ref, buf_ref, w_ref, b_ref = refs[:4]
    more_ref = refs[4] if appending else None
    y_ref, tail_ref, xx_ref = refs[-3:]
    ti = pl.program_id(2)

    @pl.when(ti == 0)
    def _():
        xx_ref[0:SUBLANES, :] = buf_ref[...]

    @pl.when(ti < nt)
    def _():
        xx_ref[SUBLANES:SUBLANES + tt, :] = x_ref[...]
        acc = b_ref[...]
        for k in range(width):
            lo = SUBLANES - (width - 1) + k
            acc = acc + w_ref[k:k + 1, :] * xx_ref[lo:lo + tt, :]
        y = _silu(acc)
        if gated:
            y = y * a_ref[...]
        y_ref[...] = y.astype(y_ref.dtype)
        tail = xx_ref[tt:tt + SUBLANES, :]
        xx_ref[0:SUBLANES, :] = tail
        tail_ref[...] = tail

    if appending:
        @pl.when(ti >= nt)
        def _():
            y_ref[...] = more_ref[...]


def _causal_conv(srcv, buf, w, b, *, boff, t, col_off, cdim, tc, tt, out_dtype, gate_col_off=None,
                 append=None):
    nb = buf.shape[0]
    width = w.shape[0]
    assert t >= SUBLANES and width - 1 <= SUBLANES
    tt = _tile(t, tt, SUBLANES)
    tc = _tile(math.gcd(cdim, col_off, gate_col_off or 0), tc, LANES)
    coff = col_off // tc
    nt = t // tt
    extra = 0 if append is None else append.shape[0]
    assert extra % tt == 0 and (extra == 0 or nb == 1)
    last = lambda k: jnp.minimum(k, nt - 1)
    buf8 = jnp.pad(buf.astype(F32), ((0, 0), (SUBLANES - (width - 1), 0), (0, 0)))
    w8 = jnp.pad(w, ((0, SUBLANES - width), (0, 0)))
    b2 = b.reshape(1, cdim)
    gated = gate_col_off is not None
    ins, specs = [], []
    if gated:
        goff = gate_col_off // tc
        ins.append(srcv)
        specs.append(pl.BlockSpec((None, tt, tc), lambda i, j, k: (i + boff, last(k), j + goff)))
    ins += [srcv, buf8, w8, b2]
    specs += [
        pl.BlockSpec((None, tt, tc), lambda i, j, k: (i + boff, last(k), j + coff)),
        pl.BlockSpec((None, SUBLANES, tc), lambda i, j, k: (i, 0, j)),
        pl.BlockSpec((SUBLANES, tc), lambda i, j, k: (0, j)),
        pl.BlockSpec((1, tc), lambda i, j, k: (0, j)),
    ]
    if extra:
        ins.append(append)
        specs.append(pl.BlockSpec((tt, tc), lambda i, j, k: (jnp.maximum(k - nt, 0), j)))
    return pl.pallas_call(
        functools.partial(_conv_kernel, width=width, tt=tt, gated=gated, nt=nt, appending=extra > 0),
        out_shape=(jax.ShapeDtypeStruct((nb, t + extra, cdim), out_dtype),
                   jax.ShapeDtypeStruct((nb, SUBLANES, cdim), F32)),
        grid=(nb, cdim // tc, nt + extra // tt),
        in_specs=specs,
        out_specs=(pl.BlockSpec((None, tt, tc), lambda i, j, k: (i, k, j)),
                   pl.BlockSpec((None, SUBLANES, tc), lambda i, j, k: (i, 0, j))),
        scratch_shapes=[pltpu.VMEM((tt + SUBLANES, tc), F32)],
        compiler_params=_params("parallel", "parallel", "arbitrary"),
        name="causal_conv_gated" if gated else "causal_conv",
    )(*ins)


def _ssd_kernel(*refs, nc, appending, **kw):
    if not appending:
        _ssd_chunk(*refs, nc=nc, **kw)
        return
    more_ref, y_ref = refs[10], refs[11]
    c = pl.program_id(2)

    @pl.when(c < nc)
    def _():
        _ssd_chunk(*(refs[:10] + refs[11:]), nc=nc, **kw)

    @pl.when(c >= nc)
    def _():
        y_ref[...] = more_ref[...]


def _ssd_chunk(x_ref, b_ref, c_ref, z_ref, dt_ref, dtb_ref, alog_ref, dsk_ref, nw_ref, h0_ref,
               y_ref, hout_ref, ht_ref, yg_ref, *, hpg, hd, ln, nheads, nc):
    g = pl.program_id(1)
    c = pl.program_id(2)
    gw = hpg * hd
    per_blk = LANES // hd
    nblk = gw // LANES

    @pl.when(c == 0)
    def _():
        ht_ref[...] = h0_ref[...].T

    dt = _softplus(dt_ref[...] + dtb_ref[...])
    a_neg = -jnp.exp(alog_ref[...])
    shift = lax.rem(nheads - g * hpg, nheads)
    dt_g = pltpu.roll(dt, shift, axis=1)
    da_g = pltpu.roll(dt * a_neg, shift, axis=1)
    dsk_g = pltpu.roll(jnp.broadcast_to(dsk_ref[...], (SUBLANES, nheads)), shift, axis=1)

    row = lax.broadcasted_iota(jnp.int32, (ln, ln), 0)
    col = lax.broadcasted_iota(jnp.int32, (ln, ln), 1)
    tril = row >= col
    a_cum = _dot01(tril, da_g)
    hp = max(hpg, SUBLANES)
    sel = (lax.broadcasted_iota(jnp.int32, (hp, nheads), 0)
           == lax.broadcasted_iota(jnp.int32, (hp, nheads), 1))
    nt = (((1,), (1,)), ((), ()))
    a_cum_t = _dot01(sel, a_cum, nt)

    bmat = b_ref[...].astype(MXU_DTYPE)
    cmat = c_ref[...].astype(MXU_DTYPE)
    cb = lax.dot_general(cmat, bmat, nt, preferred_element_type=F32)

    lane = lax.broadcasted_iota(jnp.int32, (1, LANES), 1)
    ssq = jnp.zeros((ln, 1), F32)
    for j in range(nblk):
        lo, hi = j * LANES, (j + 1) * LANES
        x_blk = x_ref[:, lo:hi]
        acol = dcol = dsk_row = None
        for w in range(per_blk):
            i = j * per_blk + w
            a_b = jnp.broadcast_to(a_cum[:, i:i + 1], (ln, LANES))
            d_b = jnp.broadcast_to(dt_g[:, i:i + 1], (ln, LANES))
            s_b = jnp.broadcast_to(dsk_g[0:1, i:i + 1], (1, LANES))
            if w == 0:
                acol, dcol, dsk_row = a_b, d_b, s_b
            else:
                in_w = lane >= w * hd
                acol = jnp.where(in_w, a_b, acol)
                dcol = jnp.where(in_w, d_b, dcol)
                dsk_row = jnp.where(in_w, s_b, dsk_row)
        alast = acol[ln - 1:ln, :]
        xdt = x_blk * dcol
        xdt_m = xdt.astype(MXU_DTYPE)
        xw_m = (xdt * jnp.exp(alast - acol)).astype(MXU_DTYPE)
        ydiag = None
        for w in range(per_blk):
            i = j * per_blk + w
            seg = acol[:, w * hd:w * hd + 1] - a_cum_t[i:i + 1, :]
            dec = jnp.exp(jnp.where(tril, seg, NEG_INF))
            yw = jnp.dot((cb * dec).astype(MXU_DTYPE), xdt_m, preferred_element_type=F32)
            ydiag = yw if w == 0 else jnp.where(lane >= w * hd, yw, ydiag)
        h_blk = ht_ref[:, lo:hi]
        yoff = jnp.dot(cmat, h_blk.astype(MXU_DTYPE), preferred_element_type=F32) * jnp.exp(acol)
        st = lax.dot_general(bmat, xw_m, (((0,), (0,)), ((), ())), preferred_element_type=F32)
        ht_ref[:, lo:hi] = h_blk * jnp.exp(alast) + st
        y = ydiag + yoff + dsk_row * x_blk
        y = y * _silu(z_ref[:, lo:hi])
        yg_ref[:, lo:hi] = y
        ssq = ssq + jnp.sum(y * y, axis=-1, keepdims=True)

    inv = lax.rsqrt(ssq * (1.0 / gw) + EPS)
    y_ref[...] = (yg_ref[...] * inv * nw_ref[...]).astype(y_ref.dtype)

    @pl.when(c == nc - 1)
    def _():
        hout_ref[...] = ht_ref[...].T


def _ssd(xbc, zxv, dtv, dt_bias, a_log, d_skip, norm_w, h0, *, boff, d_inner, ngroups, nstate, chunk,
         append=None):
    nb, t, _ = xbc.shape
    nheads = dtv.shape[-1]
    hd = d_inner // nheads
    hpg = nheads // ngroups
    gw = hpg * hd
    assert nheads <= LANES
    assert LANES % hd == 0 and gw % LANES == 0 and nstate % LANES == 0 and t % chunk == 0
    nc = t // chunk
    h0g = h0.astype(F32).reshape(nb, ngroups, gw, nstate)
    vec = lambda a: a.astype(F32).reshape(1, nheads)
    xoff = d_inner // nstate
    extra = 0 if append is None else append.shape[0]
    assert extra % chunk == 0 and (extra == 0 or nb == 1)
    last = lambda c: jnp.minimum(c, nc - 1)
    kern = functools.partial(_ssd_kernel, hpg=hpg, hd=hd, ln=chunk, nheads=nheads, nc=nc,
                             appending=extra > 0)
    ins = [xbc, xbc, xbc, zxv, dtv, vec(dt_bias), vec(a_log), vec(d_skip),
           norm_w.astype(F32).reshape(1, d_inner), h0g]
    specs = [
        pl.BlockSpec((None, chunk, gw), lambda b, g, c: (b, last(c), g)),
        pl.BlockSpec((None, chunk, nstate), lambda b, g, c: (b, last(c), xoff + g)),
        pl.BlockSpec((None, chunk, nstate), lambda b, g, c: (b, last(c), xoff + ngroups + g)),
        pl.BlockSpec((None, chunk, gw), lambda b, g, c: (b + boff, last(c), g)),
        pl.BlockSpec((None, chunk, nheads), lambda b, g, c: (b + boff, last(c), 0)),
        pl.BlockSpec((1, nheads), lambda b, g, c: (0, 0)),
        pl.BlockSpec((1, nheads), lambda b, g, c: (0, 0)),
        pl.BlockSpec((1, nheads), lambda b, g, c: (0, 0)),
        pl.BlockSpec((1, gw), lambda b, g, c: (0, g)),
        pl.BlockSpec((None, None, gw, nstate), lambda b, g, c: (b, g, 0, 0)),
    ]
    if extra:
        ins.append(append)
        specs.append(pl.BlockSpec((chunk, gw), lambda b, g, c: (jnp.maximum(c - nc, 0), g)))
    y, hout = pl.pallas_call(
        kern,
        out_shape=(jax.ShapeDtypeStruct((nb, t + extra, d_inner), MXU_DTYPE),
                   jax.ShapeDtypeStruct((nb, ngroups, gw, nstate), F32)),
        grid=(nb, ngroups, nc + extra // chunk),
        in_specs=specs,
        out_specs=(pl.BlockSpec((None, chunk, gw), lambda b, g, c: (b, c, g)),
                   pl.BlockSpec((None, None, gw, nstate), lambda b, g, c: (b, g, 0, 0))),
        scratch_shapes=[pltpu.VMEM((nstate, gw), F32), pltpu.VMEM((chunk, gw), F32)],
        compiler_params=_params("parallel", "parallel", "arbitrary"),
        name="ssd_scan",
    )(*ins)
    return y, hout.reshape(nb, nheads, hd, nstate)


def _cumsum_kernel(lf_ref, c_ref, ct_ref, carry_ref, *, tb):
    @pl.when(pl.program_id(1) == 0)
    def _():
        carry_ref[...] = jnp.zeros_like(carry_ref)

    row = lax.broadcasted_iota(jnp.int32, (tb, tb), 0)
    col = lax.broadcasted_iota(jnp.int32, (tb, tb), 1)
    cs = _dot01(row >= col, lf_ref[...]) + carry_ref[0:1, :]
    c_ref[...] = cs
    ct_ref[...] = cs.T
    carry_ref[...] = jnp.broadcast_to(cs[tb - 1:tb, :], carry_ref.shape)


def _cumsum_time(lfv, *, t, tb):
    bsz, _, w = lfv.shape
    return pl.pallas_call(
        functools.partial(_cumsum_kernel, tb=tb),
        out_shape=(jax.ShapeDtypeStruct((bsz, t, w), F32), jax.ShapeDtypeStruct((bsz, w, t), F32)),
        grid=(bsz, t // tb),
        in_specs=[pl.BlockSpec((None, tb, w), lambda b, i: (b, i, 0))],
        out_specs=(pl.BlockSpec((None, tb, w), lambda b, i: (b, i, 0)),
                   pl.BlockSpec((None, w, tb), lambda b, i: (b, 0, i))),
        scratch_shapes=[pltpu.VMEM((SUBLANES, w), F32)],
        compiler_params=_params("parallel", "arbitrary"),
        name="logf_cumsum",
    )(lfv)


def _head_column(c_blk, h, width):
    rolled = pltpu.roll(c_blk, lax.rem(LANES - h, LANES), axis=1)
    return jnp.broadcast_to(rolled[:, 0:1], (c_blk.shape[0], width))


def _aug_cols(c_blk, h, for_query):
    col = _head_column(c_blk, h, LANES) * LOG2E
    c1, c2, c3 = (t.astype(F32) for t in _split3(col))
    one = jnp.ones_like(col)
    terms = (c1, c2, c3, one, one, one) if for_query else (one, one, one, -c1, -c2, -c3)
    lane = lax.broadcasted_iota(jnp.int32, col.shape, 1)
    out = jnp.zeros_like(col)
    for idx, term in enumerate(terms):
        out = jnp.where(lane == idx, term, out)
    return out


def _fox_prep_kernel(q_ref, k_ref, v_ref, c_ref, qat_ref, ka_ref, vt_ref):
    h = pl.program_id(1)
    dh = q_ref.shape[-1]
    c_blk = c_ref[...]
    ka_ref[:, 0:dh] = k_ref[...].astype(MXU_DTYPE)
    ka_ref[:, dh:dh + LANES] = _aug_cols(c_blk, h, False).astype(MXU_DTYPE)
    qat_ref[0:dh, :] = q_ref[...].astype(F32).T.astype(MXU_DTYPE)
    qat_ref[dh:dh + LANES, :] = _aug_cols(c_blk, h, True).T.astype(MXU_DTYPE)
    vt_ref[...] = v_ref[...].T.astype(MXU_DTYPE)


def _fox_prep(qv, kv, vv, c, *, t, dh):
    bsz, _, d = qv.shape
    nh = d // dh
    tt = _tile(t, 512, LANES)
    ka_w = dh + LANES
    return pl.pallas_call(
        _fox_prep_kernel,
        out_shape=(jax.ShapeDtypeStruct((bsz, nh, ka_w, t), MXU_DTYPE),
                   jax.ShapeDtypeStruct((bsz, nh, t, ka_w), MXU_DTYPE),
                   jax.ShapeDtypeStruct((bsz, nh, dh, t), MXU_DTYPE)),
        grid=(bsz, nh, t // tt),
        in_specs=[
            pl.BlockSpec((None, tt, dh), lambda b, h, i: (b, i, h)),
            pl.BlockSpec((None, tt, dh), lambda b, h, i: (b, i, h)),
            pl.BlockSpec((None, tt, dh), lambda b, h, i: (b, i, h)),
            pl.BlockSpec((None, tt, LANES), lambda b, h, i: (b, i, 0)),
        ],
        out_specs=(pl.BlockSpec((None, None, ka_w, tt), lambda b, h, i: (b, h, 0, i)),
                   pl.BlockSpec((None, None, tt, ka_w), lambda b, h, i: (b, h, i, 0)),
                   pl.BlockSpec((None, None, dh, tt), lambda b, h, i: (b, h, 0, i))),
        compiler_params=_params("parallel", "parallel", "parallel"),
        name="fox_prep",
    )(qv, kv, vv, c)


def _fox_prefill_kernel(qat_ref, ka_ref, vt_ref, o_ref, m_ref, l_ref, acc_ref, *, tq):
    qi = pl.program_id(2)
    ki = pl.program_id(3)

    @pl.when(ki == 0)
    def _():
        m_ref[...] = jnp.full_like(m_ref, NEG_INF)
        l_ref[...] = jnp.zeros_like(l_ref)
        acc_ref[...] = jnp.zeros_like(acc_ref)

    def step(diagonal):
        s = jnp.dot(ka_ref[...], qat_ref[...], preferred_element_type=F32)
        if diagonal:
            krow = lax.broadcasted_iota(jnp.int32, (tq, tq), 0)
            qcol = lax.broadcasted_iota(jnp.int32, (tq, tq), 1)
            s = jnp.where(krow <= qcol, s, NEG_INF)
        m_prev = m_ref[...]
        m_new = jnp.maximum(m_prev, jnp.max(s, axis=0, keepdims=True))
        p = jnp.exp2(s - m_new)
        alpha = jnp.exp2(m_prev - m_new)
        l_ref[...] = alpha * l_ref[...] + jnp.sum(p, axis=0, keepdims=True)
        acc_ref[...] = alpha * acc_ref[...] + jnp.dot(
            vt_ref[...], p.astype(MXU_DTYPE), preferred_element_type=F32)
        m_ref[...] = m_new

    @pl.when(ki < qi)
    def _():
        step(False)

    @pl.when(ki == qi)
    def _():
        step(True)

    @pl.when(ki == pl.num_programs(3) - 1)
    def _():
        o_ref[...] = (acc_ref[...] / l_ref[...]).T.astype(o_ref.dtype)


def _fox_prefill(qat, ka, vt, *, dh):
    bsz, nh, ka_w, t = qat.shape
    tq = _tile(t, 1024, LANES)
    nq = t // tq
    kclamp = lambda qi, ki: jnp.minimum(ki, qi)
    return pl.pallas_call(
        functools.partial(_fox_prefill_kernel, tq=tq),
        out_shape=jax.ShapeDtypeStruct((bsz, t, nh * dh), MXU_DTYPE),
        grid=(bsz, nh, nq, nq),
        in_specs=[
            pl.BlockSpec((None, None, ka_w, tq), lambda b, h, qi, ki: (b, h, 0, qi)),
            pl.BlockSpec((None, None, tq, ka_w), lambda b, h, qi, ki: (b, h, kclamp(qi, ki), 0)),
            pl.BlockSpec((None, None, dh, tq), lambda b, h, qi, ki: (b, h, 0, kclamp(qi, ki))),
        ],
        out_specs=pl.BlockSpec((None, tq, dh), lambda b, h, qi, ki: (b, qi, h)),
        scratch_shapes=[pltpu.VMEM((1, tq), F32), pltpu.VMEM((1, tq), F32), pltpu.VMEM((dh, tq), F32)],
        compiler_params=_params("parallel", "parallel", "parallel", "arbitrary"),
        name="fox_prefill",
    )(qat, ka, vt)


def _fox_decode_kernel(q_ref, kp_ref, vp_ref, kn_ref, vn_ref, cq_ref, ckt_ref, o_ref, *, past, tq):
    h = pl.program_id(1)
    nt = (((1,), (1,)), ((), ()))
    q = q_ref[...]
    ck = ckt_ref[pl.ds(h, 1), :]
    cq_p = _head_column(cq_ref[...], h, past)
    s_p = lax.dot_general(q, kp_ref[...].astype(MXU_DTYPE), nt, preferred_element_type=F32)
    s_p = s_p + (cq_p - ck[:, 0:past]) * LOG2E
    s_n = lax.dot_general(q, kn_ref[...].astype(MXU_DTYPE), nt, preferred_element_type=F32)
    s_n = s_n + (cq_p[:, 0:tq] - ck[:, past:past + tq]) * LOG2E
    row = lax.broadcasted_iota(jnp.int32, (tq, tq), 0)
    col = lax.broadcasted_iota(jnp.int32, (tq, tq), 1)
    s_n = jnp.where(col <= row, s_n, NEG_INF)
    m = jnp.maximum(jnp.max(s_p, axis=-1, keepdims=True), jnp.max(s_n, axis=-1, keepdims=True))
    p_p = jnp.exp2(s_p - m)
    p_n = jnp.exp2(s_n - m)
    l = jnp.sum(p_p, axis=-1, keepdims=True) + jnp.sum(p_n, axis=-1, keepdims=True)
    acc = jnp.dot(p_p.astype(MXU_DTYPE), vp_ref[...].astype(MXU_DTYPE), preferred_element_type=F32)
    acc = acc + jnp.dot(p_n.astype(MXU_DTYPE), vn_ref[...].astype(MXU_DTYPE), preferred_element_type=F32)
    o_ref[...] = (acc / l).astype(o_ref.dtype)


def _fox_decode(qv, k_past, v_past, knv, vnv, c, ct, *, boff, t, dh):
    nb, past, d = k_past.shape
    nh = d // dh
    tpad = ct.shape[-1]
    assert past % t == 0 and past % LANES == 0
    new_blk = lambda b, h: (b + boff, 0, h)
    return pl.pallas_call(
        functools.partial(_fox_decode_kernel, past=past, tq=t),
        out_shape=jax.ShapeDtypeStruct((nb, t, d), MXU_DTYPE),
        grid=(nb, nh),
        in_specs=[
            pl.BlockSpec((None, t, dh), new_blk),
            pl.BlockSpec((None, past, dh), lambda b, h: (b, 0, h)),
            pl.BlockSpec((None, past, dh), lambda b, h: (b, 0, h)),
            pl.BlockSpec((None, t, dh), new_blk),
            pl.BlockSpec((None, t, dh), new_blk),
            pl.BlockSpec((None, t, LANES), lambda b, h: (b, past // t, 0)),
            pl.BlockSpec((None, LANES, tpad), lambda b, h: (b, 0, 0)),
        ],
        out_specs=pl.BlockSpec((None, t, dh), lambda b, h: (b, 0, h)),
        compiler_params=_params("parallel", "parallel"),
        name="fox_decode",
    )(qv, k_past, v_past, knv, vnv, c, ct)


def _forward(x, mod, passes, bounds, k_past, v_past, lf_past, p):
    r, d = x.shape
    depth = p["w_ada"].shape[0]
    n_a = p["m_w_in"].shape[0]
    d_inner = p["m_w_out"].shape[1]
    conv_dim = p["m_conv_w"].shape[2]
    nheads = p["m_dt_bias"].shape[1]
    nstate = passes[0]["ssm0"].shape[-1]
    ngroups = (conv_dim - d_inner) // (2 * nstate)
    d_ff = p["f_w_down"].shape[1]
    nh_fox, dh = k_past.shape[2], k_past.shape[3]
    past = k_past.shape[1]

    def tables(s):
        return tuple(mod[s, :, i * d:(i + 1) * d] for i in range(3))

    def stack_rows(parts):
        return jnp.concatenate([a.reshape(-1, a.shape[-1]) for a in parts], axis=0)

    def stacked(stage, row_tile):
        rest = [stage(ps, out, None) for ps, out in zip(passes[1:], outs[1:])]
        more = stack_rows(rest)
        if passes[0]["nb"] == 1 and more.shape[0] % row_tile == 0:
            return stage(passes[0], outs[0], more)[0]
        return stack_rows([stage(passes[0], outs[0], None)] + rest)

    outs = [dict(ssm=[], conv=[], ffn=[]) for _ in passes]
    k_all = v_all = lf_pad = None
    cums = [None] * len(passes)
    for l in range(depth):
        shift, scale, gate = tables(2 * l)
        h = _norm_mod(x, scale, shift, bounds)
        if l < n_a:
            w_in = p["m_w_in"]
            zx = _matmul(h, w_in, layer=l, n=d_inner + conv_dim, bn=512)
            dt_raw = _matmul(h, w_in, layer=l, n_off=d_inner + conv_dim, n=nheads, bn=nheads)
            def mixer(ps, out, append):
                zxv, dtv = ps["view"](zx), ps["view"](dt_raw)
                xbc, tail = _causal_conv(zxv, ps["conv0"][l], p["m_conv_w"][l], p["m_conv_b"][l],
                                         boff=ps["boff"], t=ps["t"], col_off=d_inner, cdim=conv_dim,
                                         tc=2048, tt=256, out_dtype=F32)
                out["conv"].append(tail[:, SUBLANES - (p["m_conv_w"].shape[1] - 1):])
                y, sst = _ssd(xbc, zxv, dtv, p["m_dt_bias"][l], p["m_a_log"][l], p["m_d"][l],
                              p["m_norm_w"][l], ps["ssm0"][l], boff=ps["boff"], d_inner=d_inner,
                              ngroups=ngroups, nstate=nstate, chunk=min(SSD_CHUNK, ps["t"]),
                              append=append)
                out["ssm"].append(sst)
                return y

            y_all = stacked(mixer, min(SSD_CHUNK, passes[0]["t"]))
            x = _matmul(y_all, p["m_w_out"], layer=l, bm=MM_ROWS_WIDE_K, bn=256, bk=d_inner,
                        epilogue="resid", res=x, gate_tbl=gate, bounds=bounds)
        else:
            j = l - n_a
            q = _matmul(h, p["w_q"], layer=j, out_dtype=MXU_DTYPE, bn=512, out_scale=dh ** -0.5 * LOG2E)
            os_ = []
            for ps, cum in zip(passes, cums):
                if ps["prefill"]:
                    qat, ka, vt = _fox_prep(ps["view"](q), ps["view"](k_all), ps["view"](v_all), cum[0],
                                            t=ps["t"], dh=dh)
                    o = _fox_prefill(qat, ka, vt, dh=dh)
                else:
                    o = _fox_decode(ps["view"](q), ps["k_past"], ps["v_past"], ps["view"](k_all),
                                    ps["view"](v_all), cum[0], cum[1], boff=ps["boff"], t=ps["t"], dh=dh)
                os_.append(o)
            x = _matmul(stack_rows(os_), p["w_o"], layer=j, bm=MM_ROWS_WIDE_K, bn=512,
                        epilogue="resid", res=x, gate_tbl=gate, bounds=bounds)
        shift, scale, gate = tables(2 * l + 1)
        h = _norm_mod(x, scale, shift, bounds)
        up = _matmul(h, p["f_w_up"], layer=l, bn=512)
        ffn_tt = 64

        def ffn_gate(ps, out, append):
            u, tail = _causal_conv(ps["view"](up), ps["ffn0"][l], p["f_conv_w"][l], p["f_conv_b"][l],
                                   boff=ps["boff"], t=ps["t"], col_off=d_ff, cdim=d_ff, tc=d_ff, tt=ffn_tt,
                                   out_dtype=MXU_DTYPE, gate_col_off=0, append=append)
            out["ffn"].append(tail[:, SUBLANES - (p["f_conv_w"].shape[1] - 1):])
            return u

        u_all = stacked(ffn_gate, _tile(passes[0]["t"], ffn_tt, SUBLANES))
        x = _matmul(u_all, p["f_w_down_lp"], layer=l, bm=MM_ROWS_WIDE_K, bn=256, bk=d_ff,
                    epilogue="resid", res=x, gate_tbl=gate, bounds=bounds)
        if l == n_a - 1:
            hkv = _norm_w(x[None], p["kv_norm_w"], boff=0, nb=1, t=r, out_dtype=MXU_DTYPE)[0]
            k_all = _matmul(hkv, p["w_kv"], n=d, bn=512)
            v_all = _matmul(hkv, p["w_kv"], n_off=d, n=d, bn=512)
            w_fg = jnp.pad(p["w_fgate"], ((0, 0), (0, LANES - nh_fox)))
            b_fg = jnp.pad(p["b_fgate"].astype(F32), (0, LANES - nh_fox)).reshape(1, LANES)
            lf_pad = _matmul(hkv, w_fg, bn=LANES, epilogue="logsig", bias=b_fg)
            tb = 256
            for i, ps in enumerate(passes):
                if ps["prefill"]:
                    assert ps["nb"] == 1 and ps["boff"] == 0 and ps["t"] % tb == 0
                    cums[i] = _cumsum_time(lf_pad[None], t=ps["t"], tb=tb)
                else:
                    lo = ps["boff"] * ps["t"]
                    lf_new = lf_pad[lo:lo + ps["nb"] * ps["t"]].reshape(ps["nb"], ps["t"], LANES)
                    lf_p = jnp.pad(ps["lf_past"].astype(F32), ((0, 0), (0, 0), (0, LANES - nh_fox)))
                    lf_all = jnp.concatenate([lf_p, lf_new], axis=1)
                    tot = lf_all.shape[1]
                    lf_all = jnp.pad(lf_all, ((0, 0), (0, (-tot) % tb), (0, 0)))
                    cums[i] = _cumsum_time(lf_all, t=lf_all.shape[1], tb=tb)
    results = []
    for ps, out in zip(passes, outs):
        lo, n = ps["boff"] * ps["t"], ps["nb"] * ps["t"]
        y_out = _norm_w(ps["view"](x), p["final_norm_w"], boff=ps["boff"], nb=ps["nb"], t=ps["t"],
                        out_dtype=x.dtype)
        seq = lambda a, w: a[lo:lo + n, :w].reshape((ps["nb"], ps["t"]) + ((nh_fox, dh) if w == d else (w,)))
        results.append((y_out, jnp.stack(out["ssm"]).astype(x.dtype), jnp.stack(out["conv"]),
                        jnp.stack(out["ffn"]), seq(k_all, d), seq(v_all, d),
                        seq(lf_pad, nh_fox).astype(x.dtype)))
    return results


def kernel(x_prompt, x_sample, c_prompt, c_sample, cache_k, cache_v, cache_logf, state_ssm, state_conv, state_ffn_conv, w_ada, b_ada, m_w_in, m_conv_w, m_conv_b, m_dt_bias, m_a_log, m_d, m_norm_w, m_w_out, kv_norm_w, w_kv, w_fgate, b_fgate, w_q, w_o, f_w_up, f_conv_w, f_conv_b, f_w_down, final_norm_w):
    p = dict(w_ada=w_ada, b_ada=b_ada, m_w_in=m_w_in, m_conv_w=m_conv_w, m_conv_b=m_conv_b,
             m_dt_bias=m_dt_bias, m_a_log=m_a_log, m_d=m_d, m_norm_w=m_norm_w, m_w_out=m_w_out,
             kv_norm_w=kv_norm_w, w_kv=w_kv, w_fgate=w_fgate, b_fgate=b_fgate, w_q=w_q, w_o=w_o,
             f_w_up=f_w_up, f_conv_w=f_conv_w, f_conv_b=f_conv_b, f_w_down=f_w_down,
             final_norm_w=final_norm_w)
    p["f_w_down_lp"] = f_w_down.astype(MXU_DTYPE)
    bp, tp, d = x_prompt.shape
    bs, ts, _ = x_sample.shape
    assert bp == 1 and tp % ts == 0 and ts % PACKED_ROWS == 0
    n_a, depth = m_w_in.shape[0], w_ada.shape[0]
    nheads, hd, nstate = state_ssm.shape[2], state_ssm.shape[3], state_ssm.shape[4]
    past = cache_k.shape[1]
    dtp = x_prompt.dtype
    r = tp + bs * ts

    nseq = bp + bs
    assert nseq <= SEQ_TABLE_ROWS
    c_rows = jnp.pad(jnp.concatenate([c_prompt, c_sample], axis=0), ((0, SEQ_TABLE_ROWS - nseq), (0, 0)))
    mod = _ada_all(c_rows, w_ada, b_ada)

    x = jnp.concatenate([x_prompt.reshape(tp, d), x_sample.reshape(bs * ts, d)], axis=0)
    bounds = ((0, tp, 0),) + tuple((tp + b * ts, tp + (b + 1) * ts, 1 + b) for b in range(bs))
    passes = [
        dict(prefill=True, boff=0, nb=1, t=tp, view=lambda a: a.reshape(1, r, a.shape[-1]),
             ssm0=jnp.zeros((n_a, bp, nheads, hd, nstate), F32),
             conv0=jnp.zeros((n_a, bp, m_conv_w.shape[1] - 1, m_conv_w.shape[2]), dtp),
             ffn0=jnp.zeros((depth, bp, f_conv_w.shape[1] - 1, f_conv_w.shape[2]), dtp)),
        dict(prefill=False, boff=tp // ts, nb=bs, t=ts, view=lambda a: a.reshape(r // ts, ts, a.shape[-1]),
             ssm0=state_ssm, conv0=state_conv, ffn0=state_ffn_conv, lf_past=cache_logf,
             k_past=cache_k.reshape(bs, past, d), v_past=cache_v.reshape(bs, past, d)),
    ]
    out_p, out_s = _forward(x, mod, passes, bounds, cache_k, cache_v, cache_logf, p)
    return (out_p[0], out_s[0]) + out_p[1:] + out_s[1:]
```

```python
import functools
import math

import jax
import jax.numpy as jnp
from jax import lax
from jax.experimental import pallas as pl
from jax.experimental.pallas import tpu as pltpu

F32 = jnp.float32
MXU_DTYPE = jnp.bfloat16
EPS = 1e-6
LANES = 128
SUBLANES = 8
PACKED_ROWS = 16
VMEM_LIMIT_BYTES = 56 * 1024 * 1024
SSD_CHUNK = 64
SSD_CHUNKS_PER_STEP = 2
NEG_INF = float("-inf")
LOG2E = math.log2(math.e)
MM_ROWS = 2080
MM_ROWS_WIDE_K = 1040
NORM_ROWS = 320
SEQ_TABLE_ROWS = 16


def _tile(dim, pref, align):
    t = (min(pref, dim) // align) * align
    while t >= align:
        if dim % t == 0:
            return t
        t -= align
    return dim


def _params(*sem):
    return pltpu.CompilerParams(dimension_semantics=sem, vmem_limit_bytes=VMEM_LIMIT_BYTES)


def _silu(x):
    return x * jax.nn.sigmoid(x)


def _softplus(x):
    return jnp.maximum(x, 0.0) + jnp.log1p(jnp.exp(-jnp.abs(x)))


def _split3(a):
    a1 = a.astype(MXU_DTYPE)
    r1 = a - a1.astype(F32)
    a2 = r1.astype(MXU_DTYPE)
    r2 = r1 - a2.astype(F32)
    return a1, a2, r2.astype(MXU_DTYPE)


def _dot01(m01, a, dims=(((1,), (0,)), ((), ()))):
    m = m01.astype(MXU_DTYPE)
    out = None
    for t in _split3(a):
        part = lax.dot_general(m, t, dims, preferred_element_type=F32)
        out = part if out is None else out + part
    return out


def _dot01r(a, m01):
    out = None
    for t in _split3(a):
        part = jnp.dot(t, m01.astype(MXU_DTYPE), preferred_element_type=F32)
        out = part if out is None else out + part
    return out


def _tile_segments(rows_per_tile, ntiles, bounds):
    out = []
    for i in range(ntiles):
        lo, hi = i * rows_per_tile, (i + 1) * rows_per_tile
        out.append(tuple((max(s, lo) - lo, min(e, hi) - lo, q) for s, e, q in bounds
                         if max(s, lo) < min(e, hi)))
    return tuple(out)


def _for_tile_segments(i, tile_segs, emit):
    groups = {}
    for t, segs in enumerate(tile_segs):
        groups.setdefault(segs, []).append(t)
    if len(groups) == 1:
        emit(tile_segs[0])
        return
    for segs, tiles in groups.items():
        runs = []
        for t in tiles:
            if runs and runs[-1][1] == t - 1:
                runs[-1][1] = t
            else:
                runs.append([t, t])
        cond = None
        for a, b in runs:
            c = (i == a) if a == b else ((i >= a) & (i <= b))
            cond = c if cond is None else (cond | c)
        pl.when(cond)(functools.partial(emit, segs))


def _ada_kernel(c_ref, w_ref, b_ref, o_ref):
    a = _silu(c_ref[...]).astype(MXU_DTYPE)
    w = w_ref[...].astype(MXU_DTYPE)
    o_ref[...] = jnp.dot(a, w, preferred_element_type=F32) + b_ref[...]


def _ada_all(c_rows, w_ada, b_ada):
    r, d = c_rows.shape
    s = w_ada.shape[0] * w_ada.shape[1]
    n = w_ada.shape[-1]
    w = w_ada.reshape(s, d, n)
    b = b_ada.reshape(s, 1, n)
    tn = _tile(n, 512, LANES)
    return pl.pallas_call(
        _ada_kernel,
        out_shape=jax.ShapeDtypeStruct((s, r, n), F32),
        grid=(s, n // tn),
        in_specs=[
            pl.BlockSpec((r, d), lambda i, j: (0, 0)),
            pl.BlockSpec((None, d, tn), lambda i, j: (i, 0, j)),
            pl.BlockSpec((None, 1, tn), lambda i, j: (i, 0, j)),
        ],
        out_specs=pl.BlockSpec((None, r, tn), lambda i, j: (i, 0, j)),
        compiler_params=_params("parallel", "parallel"),
        name="ada_mod",
    )(c_rows, w, b)


def _rms(x):
    return x * lax.rsqrt(jnp.mean(x * x, axis=-1, keepdims=True) + EPS)


def _norm_mod_kernel(x_ref, scale_ref, shift_ref, o_ref, *, tile_segs):
    y = _rms(x_ref[...])

    def emit(segs):
        for lo, hi, s in segs:
            v = y[lo:hi, :] * (1.0 + scale_ref[s:s + 1, :]) + shift_ref[s:s + 1, :]
            o_ref[lo:hi, :] = v.astype(o_ref.dtype)

    _for_tile_segments(pl.program_id(0), tile_segs, emit)


def _norm_mod(x, scale_tbl, shift_tbl, bounds):
    r, d = x.shape
    tt = _tile(r, NORM_ROWS, PACKED_ROWS)
    nt = r // tt
    s = scale_tbl.shape[0]
    return pl.pallas_call(
        functools.partial(_norm_mod_kernel, tile_segs=_tile_segments(tt, nt, bounds)),
        out_shape=jax.ShapeDtypeStruct((r, d), MXU_DTYPE),
        grid=(nt,),
        in_specs=[pl.BlockSpec((tt, d), lambda i: (i, 0)),
                  pl.BlockSpec((s, d), lambda i: (0, 0)),
                  pl.BlockSpec((s, d), lambda i: (0, 0))],
        out_specs=pl.BlockSpec((tt, d), lambda i: (i, 0)),
        compiler_params=_params("parallel"),
        name="rmsnorm_mod",
    )(x, scale_tbl, shift_tbl)


def _norm_w_kernel(x_ref, w_ref, o_ref):
    o_ref[...] = (_rms(x_ref[...]) * w_ref[...]).astype(o_ref.dtype)


def _norm_w(xv, w, *, boff, nb, t, out_dtype):
    d = xv.shape[-1]
    tt = _tile(t, NORM_ROWS, PACKED_ROWS)
    return pl.pallas_call(
        _norm_w_kernel,
        out_shape=jax.ShapeDtypeStruct((nb, t, d), out_dtype),
        grid=(nb, t // tt),
        in_specs=[pl.BlockSpec((None, tt, d), lambda i, j: (i + boff, j, 0)),
                  pl.BlockSpec((1, d), lambda i, j: (0, 0))],
        out_specs=pl.BlockSpec((None, tt, d), lambda i, j: (i, j, 0)),
        compiler_params=_params("parallel", "parallel"),
        name="rmsnorm_w",
    )(xv, w.astype(F32).reshape(1, d))


def _mm_kernel(*refs, nk, epilogue, out_scale, tile_segs):
    x_ref, w_ref = refs[0], refs[1]
    pos = 2
    if epilogue == "resid":
        res_ref, gate_ref = refs[2], refs[3]
        pos = 4
    elif epilogue == "logsig":
        bias_ref = refs[2]
        pos = 3
    o_ref = refs[pos]
    part = jnp.dot(x_ref[...], w_ref[...].astype(MXU_DTYPE), preferred_element_type=F32)

    def finish(acc):
        if out_scale is not None:
            acc = acc * out_scale
        if epilogue == "resid":
            def emit(segs):
                for lo, hi, s in segs:
                    v = res_ref[lo:hi, :] + gate_ref[s:s + 1, :] * acc[lo:hi, :]
                    o_ref[lo:hi, :] = v.astype(o_ref.dtype)

            _for_tile_segments(pl.program_id(0), tile_segs, emit)
            return
        if epilogue == "logsig":
            acc = -_softplus(-(acc + bias_ref[...]))
        o_ref[...] = acc.astype(o_ref.dtype)

    if nk == 1:
        finish(part)
    else:
        acc_ref = refs[pos + 1]
        k = pl.program_id(2)

        @pl.when(k == 0)
        def _():
            acc_ref[...] = part

        @pl.when(k > 0)
        def _():
            acc_ref[...] += part

        @pl.when(k == nk - 1)
        def _():
            finish(acc_ref[...])


def _matmul(x, w, *, layer=0, n_off=0, n=None, out_dtype=F32, bm=None, bn=512, bk=4096,
            epilogue=None, res=None, gate_tbl=None, bounds=None, bias=None, out_scale=None):
    m, kdim = x.shape
    if w.ndim == 2:
        w = w[None]
    n = w.shape[2] if n is None else n
    bm = _tile(m, MM_ROWS if bm is None else bm, PACKED_ROWS)
    bn = _tile(math.gcd(n, n_off), bn, LANES)
    bk = _tile(kdim, bk, LANES)
    joff = n_off // bn
    nk = kdim // bk
    ins = [x, w]
    x_mode = dict(pipeline_mode=pl.Buffered(1)) if nk == 1 and n // bn > 1 else {}
    specs = [pl.BlockSpec((bm, bk), lambda i, j, k: (i, k), **x_mode),
             pl.BlockSpec((None, bk, bn), lambda i, j, k: (layer, k, j + joff))]
    tile_segs = None
    if epilogue == "resid":
        ins += [res, gate_tbl]
        specs.append(pl.BlockSpec((bm, bn), lambda i, j, k: (i, j)))
        specs.append(pl.BlockSpec((gate_tbl.shape[0], bn), lambda i, j, k: (0, j)))
        tile_segs = _tile_segments(bm, m // bm, bounds)
    elif epilogue == "logsig":
        ins.append(bias)
        specs.append(pl.BlockSpec((1, bn), lambda i, j, k: (0, j)))
    scratch = [pltpu.VMEM((bm, bn), F32)] if nk > 1 else []
    return pl.pallas_call(
        functools.partial(_mm_kernel, nk=nk, epilogue=epilogue, out_scale=out_scale,
                          tile_segs=tile_segs),
        out_shape=jax.ShapeDtypeStruct((m, n), out_dtype),
        grid=(m // bm, n // bn, nk),
        in_specs=specs,
        out_specs=pl.BlockSpec((bm, bn), lambda i, j, k: (i, j)),
        scratch_shapes=scratch,
        compiler_params=_params("parallel", "parallel", "arbitrary"),
        name="matmul_" + (epilogue or "plain"),
    )(*ins)


def _conv_kernel(*refs, width, tt, gated, nt, appending):
    refs = list(refs)
    a_ref = refs.pop(0) if gated else None
    x_ref, buf_ref, w_ref, b_ref = refs[:4]
    more_ref = refs[4] if appending else None
    y_ref, tail_ref, xx_ref = refs[-3:]
    ti = pl.program_id(2)

    @pl.when(ti == 0)
    def _():
        xx_ref[0:SUBLANES, :] = buf_ref[...]

    @pl.when(ti < nt)
    def _():
        xx_ref[SUBLANES:SUBLANES + tt, :] = x_ref[...]
        acc = b_ref[...]
        for k in range(width):
            lo = SUBLANES - (width - 1) + k
            acc = acc + w_ref[k:k + 1, :] * xx_ref[lo:lo + tt, :]
        y = _silu(acc)
        if gated:
            y = y * a_ref[...]
        y_ref[...] = y.astype(y_ref.dtype)
        tail = xx_ref[tt:tt + SUBLANES, :]
        xx_ref[0:SUBLANES, :] = tail
        tail_ref[...] = tail

    if appending:
        @pl.when(ti >= nt)
        def _():
            y_ref[...] = more_ref[...]


def _causal_conv(srcv, buf, w, b, *, boff, t, col_off, cdim, tc, tt, out_dtype, gate_col_off=None,
                 append=None):
    nb = buf.shape[0]
    width = w.shape[0]
    assert t >= SUBLANES and width - 1 <= SUBLANES
    tt = _tile(t, tt, SUBLANES)
    tc = _tile(math.gcd(cdim, col_off, gate_col_off or 0), tc, LANES)
    coff = col_off // tc
    nt = t // tt
    extra = 0 if append is None else append.shape[0]
    assert extra % tt == 0 and (extra == 0 or nb == 1)
    last = lambda k: jnp.minimum(k, nt - 1)
    buf8 = jnp.pad(buf.astype(F32), ((0, 0), (SUBLANES - (width - 1), 0), (0, 0)))
    w8 = jnp.pad(w, ((0, SUBLANES - width), (0, 0)))
    b2 = b.reshape(1, cdim)
    gated = gate_col_off is not None
    ins, specs = [], []
    if gated:
        goff = gate_col_off // tc
        ins.append(srcv)
        specs.append(pl.BlockSpec((None, tt, tc), lambda i, j, k: (i + boff, last(k), j + goff)))
    ins += [srcv, buf8, w8, b2]
    specs += [
        pl.BlockSpec((None, tt, tc), lambda i, j, k: (i + boff, last(k), j + coff)),
        pl.BlockSpec((None, SUBLANES, tc), lambda i, j, k: (i, 0, j)),
        pl.BlockSpec((SUBLANES, tc), lambda i, j, k: (0, j)),
        pl.BlockSpec((1, tc), lambda i, j, k: (0, j)),
    ]
    if extra:
        ins.append(append)
        specs.append(pl.BlockSpec((tt, tc), lambda i, j, k: (jnp.maximum(k - nt, 0), j)))
    return pl.pallas_call(
        functools.partial(_conv_kernel, width=width, tt=tt, gated=gated, nt=nt, appending=extra > 0),
        out_shape=(jax.ShapeDtypeStruct((nb, t + extra, cdim), out_dtype),
                   jax.ShapeDtypeStruct((nb, SUBLANES, cdim), F32)),
        grid=(nb, cdim // tc, nt + extra // tt),
        in_specs=specs,
        out_specs=(pl.BlockSpec((None, tt, tc), lambda i, j, k: (i, k, j)),
                   pl.BlockSpec((None, SUBLANES, tc), lambda i, j, k: (i, 0, j))),
        scratch_shapes=[pltpu.VMEM((tt + SUBLANES, tc), F32)],
        compiler_params=_params("parallel", "parallel", "arbitrary"),
        name="causal_conv_gated" if gated else "causal_conv",
    )(*ins)


def _ssd_kernel(*refs, nc, appending, **kw):
    if not appending:
        _ssd_chunk(*refs, nc=nc, **kw)
        return
    more_ref, y_ref = refs[11], refs[12]
    c = pl.program_id(2)

    @pl.when(c < nc)
    def _():
        _ssd_chunk(*(refs[:11] + refs[12:]), nc=nc, **kw)

    @pl.when(c >= nc)
    def _():
        y_ref[...] = more_ref[...]


def _ssd_chunk(x_ref, b_ref, c_ref, z_ref, dt_ref, dtb_ref, alog_ref, dsk_ref, nw_ref, e_ref, h0_ref,
               y_ref, hout_ref, ht_ref, yg_ref, *, hpg, hd, ln, nheads, nc, cps):
    g = pl.program_id(1)
    c = pl.program_id(2)
    gw = hpg * hd
    per_blk = LANES // hd
    nblk = gw // LANES
    nblk_p = max(nblk, SUBLANES)
    nt = (((1,), (1,)), ((), ()))

    @pl.when(c == 0)
    def _():
        ht_ref[...] = h0_ref[...].T

    a_neg = -jnp.exp(alog_ref[...])
    shift = lax.rem(nheads - g * hpg, nheads)
    dsk_g = pltpu.roll(jnp.broadcast_to(dsk_ref[...], (SUBLANES, nheads)), shift, axis=1)
    expand = e_ref[...]

    tril = (lax.broadcasted_iota(jnp.int32, (ln, ln), 0) >= lax.broadcasted_iota(jnp.int32, (ln, ln), 1))
    rowi = lax.broadcasted_iota(jnp.int32, (ln, LANES), 0)
    lanei = lax.broadcasted_iota(jnp.int32, (ln, LANES), 1)
    in_slot = lanei & (hd - 1)
    slot_causal = (in_slot <= rowi) & (in_slot < ln)
    slot_of_lane = [(lanei >= w * hd) & (lanei < (w + 1) * hd) for w in range(per_blk)]
    sel_blk = [lax.broadcasted_iota(jnp.int32, (nblk_p, nheads), 1)
               == per_blk * lax.broadcasted_iota(jnp.int32, (nblk_p, nheads), 0) + w
               for w in range(per_blk)]

    def in_slots(parts):
        rows = []
        for part in parts:
            rows.append(part)
            if hd > ln:
                rows.append(jnp.zeros((hd - ln, part.shape[1]), part.dtype))
        return jnp.concatenate(rows, axis=0)

    for k in range(cps):
        r0 = k * ln
        dt = _softplus(dt_ref[r0:r0 + ln, :] + dtb_ref[...])
        dt_g = pltpu.roll(dt, shift, axis=1)
        da_g = pltpu.roll(dt * a_neg, shift, axis=1)
        a_cum = _dot01(tril, da_g)
        ex = _dot01r(jnp.concatenate([a_cum, dt_g, dsk_g], axis=0), expand)
        acol_all, dcol_all, dsk_all = ex[0:ln], ex[ln:2 * ln], ex[2 * ln:2 * ln + 1]
        a_rows = None
        zero_rows = jnp.zeros((ln, nheads), F32)
        for w in range(per_blk):
            placed = in_slots([a_cum if v == w else zero_rows for v in range(per_blk)])
            part = _dot01(sel_blk[w], placed, nt)
            a_rows = part if a_rows is None else a_rows + part

        bmat = b_ref[r0:r0 + ln, :].astype(MXU_DTYPE)
        cmat = c_ref[r0:r0 + ln, :].astype(MXU_DTYPE)
        cb = lax.dot_general(cmat, in_slots([bmat] * per_blk), nt,
                             preferred_element_type=F32)

        ysq = jnp.zeros((ln, LANES), F32)
        for j in range(nblk):
            lo, hi = j * LANES, (j + 1) * LANES
            x_blk = x_ref[r0:r0 + ln, lo:hi]
            acol, dcol = acol_all[:, lo:hi], dcol_all[:, lo:hi]
            alast = acol[ln - 1:ln, :]
            xdt = x_blk * dcol
            xw_m = (xdt * jnp.exp(alast - acol)).astype(MXU_DTYPE)
            dec = jnp.exp(jnp.where(slot_causal, acol - a_rows[j:j + 1, :], NEG_INF))
            xbd = in_slots([jnp.where(slot_of_lane[w], xdt, 0.0) for w in range(per_blk)])
            xbd = xbd.astype(MXU_DTYPE)
            ydiag = jnp.dot((cb * dec).astype(MXU_DTYPE), xbd, preferred_element_type=F32)
            h_blk = ht_ref[:, lo:hi]
            yoff = jnp.dot(cmat, h_blk.astype(MXU_DTYPE), preferred_element_type=F32) * jnp.exp(acol)
            st = lax.dot_general(bmat, xw_m, (((0,), (0,)), ((), ())), preferred_element_type=F32)
            ht_ref[:, lo:hi] = h_blk * jnp.exp(alast) + st
            y = ydiag + yoff + dsk_all[:, lo:hi] * x_blk
            y = y * _silu(z_ref[r0:r0 + ln, lo:hi])
            yg_ref[r0:r0 + ln, lo:hi] = y
            ysq = ysq + y * y

        inv = lax.rsqrt(jnp.sum(ysq, axis=-1, keepdims=True) * (1.0 / gw) + EPS)
        y_ref[r0:r0 + ln, :] = (yg_ref[r0:r0 + ln, :] * inv * nw_ref[...]).astype(y_ref.dtype)

    @pl.when(c == nc - 1)
    def _():
        hout_ref[...] = ht_ref[...].T


def _ssd_rows_per_step(t, chunk):
    cps = SSD_CHUNKS_PER_STEP if t % (SSD_CHUNKS_PER_STEP * chunk) == 0 else 1
    return cps * chunk


def _ssd(xbc, zxv, dtv, dt_bias, a_log, d_skip, norm_w, h0, *, boff, d_inner, ngroups, nstate, chunk,
         append=None):
    nb, t, _ = xbc.shape
    nheads = dtv.shape[-1]
    hd = d_inner // nheads
    hpg = nheads // ngroups
    gw = hpg * hd
    assert nheads <= LANES
    assert LANES % hd == 0 and gw % LANES == 0 and nstate % LANES == 0 and t % chunk == 0
    assert hd & (hd - 1) == 0 and chunk <= hd
    rows = _ssd_rows_per_step(t, chunk)
    cps = rows // chunk
    nc = t // rows
    h0g = h0.astype(F32).reshape(nb, ngroups, gw, nstate)
    vec = lambda a: a.astype(F32).reshape(1, nheads)
    xoff = d_inner // nstate
    extra = 0 if append is None else append.shape[0]
    assert extra % rows == 0 and (extra == 0 or nb == 1)
    last = lambda c: jnp.minimum(c, nc - 1)
    kern = functools.partial(_ssd_kernel, hpg=hpg, hd=hd, ln=chunk, nheads=nheads, nc=nc, cps=cps,
                             appending=extra > 0)
    expand = (jnp.arange(nheads)[:, None] == jnp.arange(gw)[None, :] // hd).astype(MXU_DTYPE)
    ins = [xbc, xbc, xbc, zxv, dtv, vec(dt_bias), vec(a_log), vec(d_skip),
           norm_w.astype(F32).reshape(1, d_inner), expand, h0g]
    specs = [
        pl.BlockSpec((None, rows, gw), lambda b, g, c: (b, last(c), g)),
        pl.BlockSpec((None, rows, nstate), lambda b, g, c: (b, last(c), xoff + g)),
        pl.BlockSpec((None, rows, nstate), lambda b, g, c: (b, last(c), xoff + ngroups + g)),
        pl.BlockSpec((None, rows, gw), lambda b, g, c: (b + boff, last(c), g)),
        pl.BlockSpec((None, rows, nheads), lambda b, g, c: (b + boff, last(c), 0)),
        pl.BlockSpec((1, nheads), lambda b, g, c: (0, 0)),
        pl.BlockSpec((1, nheads), lambda b, g, c: (0, 0)),
        pl.BlockSpec((1, nheads), lambda b, g, c: (0, 0)),
        pl.BlockSpec((1, gw), lambda b, g, c: (0, g)),
        pl.BlockSpec((nheads, gw), lambda b, g, c: (0, 0)),
        pl.BlockSpec((None, None, gw, nstate), lambda b, g, c: (b, g, 0, 0)),
    ]
    if extra:
        ins.append(append)
        specs.append(pl.BlockSpec((rows, gw), lambda b, g, c: (jnp.maximum(c - nc, 0), g)))
    y, hout = pl.pallas_call(
        kern,
        out_shape=(jax.ShapeDtypeStruct((nb, t + extra, d_inner), MXU_DTYPE),
                   jax.ShapeDtypeStruct((nb, ngroups, gw, nstate), F32)),
        grid=(nb, ngroups, nc + extra // rows),
        in_specs=specs,
        out_specs=(pl.BlockSpec((None, rows, gw), lambda b, g, c: (b, c, g)),
                   pl.BlockSpec((None, None, gw, nstate), lambda b, g, c: (b, g, 0, 0))),
        scratch_shapes=[pltpu.VMEM((nstate, gw), F32), pltpu.VMEM((rows, gw), F32)],
        compiler_params=_params("parallel", "parallel", "arbitrary"),
        name="ssd_scan",
    )(*ins)
    return y, hout.reshape(nb, nheads, hd, nstate)


def _cumsum_kernel(lf_ref, c_ref, ct_ref, carry_ref, *, tb):
    @pl.when(pl.program_id(1) == 0)
    def _():
        carry_ref[...] = jnp.zeros_like(carry_ref)

    row = lax.broadcasted_iota(jnp.int32, (tb, tb), 0)
    col = lax.broadcasted_iota(jnp.int32, (tb, tb), 1)
    cs = _dot01(row >= col, lf_ref[...]) + carry_ref[0:1, :]
    c_ref[...] = cs
    ct_ref[...] = cs.T
    carry_ref[...] = jnp.broadcast_to(cs[tb - 1:tb, :], carry_ref.shape)


def _cumsum_time(lfv, *, t, tb):
    bsz, _, w = lfv.shape
    return pl.pallas_call(
        functools.partial(_cumsum_kernel, tb=tb),
        out_shape=(jax.ShapeDtypeStruct((bsz, t, w), F32), jax.ShapeDtypeStruct((bsz, w, t), F32)),
        grid=(bsz, t // tb),
        in_specs=[pl.BlockSpec((None, tb, w), lambda b, i: (b, i, 0))],
        out_specs=(pl.BlockSpec((None, tb, w), lambda b, i: (b, i, 0)),
                   pl.BlockSpec((None, w, tb), lambda b, i: (b, 0, i))),
        scratch_shapes=[pltpu.VMEM((SUBLANES, w), F32)],
        compiler_params=_params("parallel", "arbitrary"),
        name="logf_cumsum",
    )(lfv)


def _head_column(c_blk, h, width):
    rolled = pltpu.roll(c_blk, lax.rem(LANES - h, LANES), axis=1)
    return jnp.broadcast_to(rolled[:, 0:1], (c_blk.shape[0], width))


def _aug_cols(c_blk, h, for_query):
    col = _head_column(c_blk, h, LANES) * LOG2E
    c1, c2, c3 = (t.astype(F32) for t in _split3(col))
    one = jnp.ones_like(col)
    terms = (c1, c2, c3, one, one, one) if for_query else (one, one, one, -c1, -c2, -c3)
    lane = lax.broadcasted_iota(jnp.int32, col.shape, 1)
    out = jnp.zeros_like(col)
    for idx, term in enumerate(terms):
        out = jnp.where(lane == idx, term, out)
    return out


def _fox_prep_kernel(q_ref, k_ref, v_ref, c_ref, qat_ref, ka_ref, vt_ref):
    h = pl.program_id(1)
    dh = q_ref.shape[-1]
    c_blk = c_ref[...]
    ka_ref[:, 0:dh] = k_ref[...].astype(MXU_DTYPE)
    ka_ref[:, dh:dh + LANES] = _aug_cols(c_blk, h, False).astype(MXU_DTYPE)
    qat_ref[0:dh, :] = q_ref[...].astype(F32).T.astype(MXU_DTYPE)
    qat_ref[dh:dh + LANES, :] = _aug_cols(c_blk, h, True).T.astype(MXU_DTYPE)
    vt_ref[...] = v_ref[...].T.astype(MXU_DTYPE)


def _fox_prep(qv, kv, vv, c, *, t, dh):
    bsz, _, d = qv.shape
    nh = d // dh
    tt = _tile(t, 1024, LANES)
    ka_w = dh + LANES
    return pl.pallas_call(
        _fox_prep_kernel,
        out_shape=(jax.ShapeDtypeStruct((bsz, nh, ka_w, t), MXU_DTYPE),
                   jax.ShapeDtypeStruct((bsz, nh, t, ka_w), MXU_DTYPE),
                   jax.ShapeDtypeStruct((bsz, nh, dh, t), MXU_DTYPE)),
        grid=(bsz, nh, t // tt),
        in_specs=[
            pl.BlockSpec((None, tt, dh), lambda b, h, i: (b, i, h)),
            pl.BlockSpec((None, tt, dh), lambda b, h, i: (b, i, h)),
            pl.BlockSpec((None, tt, dh), lambda b, h, i: (b, i, h)),
            pl.BlockSpec((None, tt, LANES), lambda b, h, i: (b, i, 0)),
        ],
        out_specs=(pl.BlockSpec((None, None, ka_w, tt), lambda b, h, i: (b, h, 0, i)),
                   pl.BlockSpec((None, None, tt, ka_w), lambda b, h, i: (b, h, i, 0)),
                   pl.BlockSpec((None, None, dh, tt), lambda b, h, i: (b, h, 0, i))),
        compiler_params=_params("parallel", "parallel", "parallel"),
        name="fox_prep",
    )(qv, kv, vv, c)


def _fox_prefill_kernel(qat_ref, ka_ref, vt_ref, o_ref, m_ref, l_ref, acc_ref, *, tq):
    qi = pl.program_id(2)
    ki = pl.program_id(3)

    @pl.when(ki == 0)
    def _():
        m_ref[...] = jnp.full_like(m_ref, NEG_INF)
        l_ref[...] = jnp.zeros_like(l_ref)
        acc_ref[...] = jnp.zeros_like(acc_ref)

    def step(diagonal):
        s = jnp.dot(ka_ref[...], qat_ref[...], preferred_element_type=F32)
        if diagonal:
            krow = lax.broadcasted_iota(jnp.int32, (tq, tq), 0)
            qcol = lax.broadcasted_iota(jnp.int32, (tq, tq), 1)
            s = jnp.where(krow <= qcol, s, NEG_INF)
        m_prev = m_ref[...]
        m_new = jnp.maximum(m_prev, jnp.max(s, axis=0, keepdims=True))
        p = jnp.exp2(s - m_new)
        alpha = jnp.exp2(m_prev - m_new)
        l_ref[...] = alpha * l_ref[...] + jnp.sum(p, axis=0, keepdims=True)
        acc_ref[...] = alpha * acc_ref[...] + jnp.dot(
            vt_ref[...], p.astype(MXU_DTYPE), preferred_element_type=F32)
        m_ref[...] = m_new

    @pl.when(ki < qi)
    def _():
        step(False)

    @pl.when(ki == qi)
    def _():
        step(True)

    @pl.when(ki == pl.num_programs(3) - 1)
    def _():
        o_ref[...] = (acc_ref[...] / l_ref[...]).T.astype(o_ref.dtype)


def _fox_prefill(qat, ka, vt, *, dh):
    bsz, nh, ka_w, t = qat.shape
    tq = _tile(t, 1024, LANES)
    nq = t // tq
    kclamp = lambda qi, ki: jnp.minimum(ki, qi)
    return pl.pallas_call(
        functools.partial(_fox_prefill_kernel, tq=tq),
        out_shape=jax.ShapeDtypeStruct((bsz, t, nh * dh), MXU_DTYPE),
        grid=(bsz, nh, nq, nq),
        in_specs=[
            pl.BlockSpec((None, None, ka_w, tq), lambda b, h, qi, ki: (b, h, 0, qi)),
            pl.BlockSpec((None, None, tq, ka_w), lambda b, h, qi, ki: (b, h, kclamp(qi, ki), 0)),
            pl.BlockSpec((None, None, dh, tq), lambda b, h, qi, ki: (b, h, 0, kclamp(qi, ki))),
        ],
        out_specs=pl.BlockSpec((None, tq, dh), lambda b, h, qi, ki: (b, qi, h)),
        scratch_shapes=[pltpu.VMEM((1, tq), F32), pltpu.VMEM((1, tq), F32), pltpu.VMEM((dh, tq), F32)],
        compiler_params=_params("parallel", "parallel", "parallel", "arbitrary"),
        name="fox_prefill",
    )(qat, ka, vt)


def _fox_decode_kernel(q_ref, kp_ref, vp_ref, kn_ref, vn_ref, cq_ref, ckt_ref, o_ref, *, past, tq):
    h = pl.program_id(1)
    nt = (((1,), (1,)), ((), ()))
    q = q_ref[...]
    ck = ckt_ref[pl.ds(h, 1), :]
    cq_p = _head_column(cq_ref[...], h, past)
    s_p = lax.dot_general(q, kp_ref[...].astype(MXU_DTYPE), nt, preferred_element_type=F32)
    s_p = s_p + (cq_p - ck[:, 0:past]) * LOG2E
    s_n = lax.dot_general(q, kn_ref[...].astype(MXU_DTYPE), nt, preferred_element_type=F32)
    s_n = s_n + (cq_p[:, 0:tq] - ck[:, past:past + tq]) * LOG2E
    row = lax.broadcasted_iota(jnp.int32, (tq, tq), 0)
    col = lax.broadcasted_iota(jnp.int32, (tq, tq), 1)
    s_n = jnp.where(col <= row, s_n, NEG_INF)
    m = jnp.maximum(jnp.max(s_p, axis=-1, keepdims=True), jnp.max(s_n, axis=-1, keepdims=True))
    p_p = jnp.exp2(s_p - m)
    p_n = jnp.exp2(s_n - m)
    l = jnp.sum(p_p, axis=-1, keepdims=True) + jnp.sum(p_n, axis=-1, keepdims=True)
    acc = jnp.dot(p_p.astype(MXU_DTYPE), vp_ref[...].astype(MXU_DTYPE), preferred_element_type=F32)
    acc = acc + jnp.dot(p_n.astype(MXU_DTYPE), vn_ref[...].astype(MXU_DTYPE), preferred_element_type=F32)
    o_ref[...] = (acc / l).astype(o_ref.dtype)


def _fox_decode(qv, k_past, v_past, knv, vnv, c, ct, *, boff, t, dh):
    nb, past, d = k_past.shape
    nh = d // dh
    tpad = ct.shape[-1]
    assert past % t == 0 and past % LANES == 0
    new_blk = lambda b, h: (b + boff, 0, h)
    return pl.pallas_call(
        functools.partial(_fox_decode_kernel, past=past, tq=t),
        out_shape=jax.ShapeDtypeStruct((nb, t, d), MXU_DTYPE),
        grid=(nb, nh),
        in_specs=[
            pl.BlockSpec((None, t, dh), new_blk),
            pl.BlockSpec((None, past, dh), lambda b, h: (b, 0, h)),
            pl.BlockSpec((None, past, dh), lambda b, h: (b, 0, h)),
            pl.BlockSpec((None, t, dh), new_blk),
            pl.BlockSpec((None, t, dh), new_blk),
            pl.BlockSpec((None, t, LANES), lambda b, h: (b, past // t, 0)),
            pl.BlockSpec((None, LANES, tpad), lambda b, h: (b, 0, 0)),
        ],
        out_specs=pl.BlockSpec((None, t, dh), lambda b, h: (b, 0, h)),
        compiler_params=_params("parallel", "parallel"),
        name="fox_decode",
    )(qv, k_past, v_past, knv, vnv, c, ct)


def _forward(x, mod, passes, bounds, k_past, v_past, lf_past, p):
    r, d = x.shape
    depth = p["w_ada"].shape[0]
    n_a = p["m_w_in"].shape[0]
    d_inner = p["m_w_out"].shape[1]
    conv_dim = p["m_conv_w"].shape[2]
    nheads = p["m_dt_bias"].shape[1]
    nstate = passes[0]["ssm0"].shape[-1]
    ngroups = (conv_dim - d_inner) // (2 * nstate)
    d_ff = p["f_w_down"].shape[1]
    nh_fox, dh = k_past.shape[2], k_past.shape[3]
    past = k_past.shape[1]

    def tables(s):
        return tuple(mod[s, :, i * d:(i + 1) * d] for i in range(3))

    def stack_rows(parts):
        return jnp.concatenate([a.reshape(-1, a.shape[-1]) for a in parts], axis=0)

    def stacked(stage, row_tile):
        rest = [stage(ps, out, None) for ps, out in zip(passes[1:], outs[1:])]
        more = stack_rows(rest)
        if passes[0]["nb"] == 1 and more.shape[0] % row_tile == 0:
            return stage(passes[0], outs[0], more)[0]
        return stack_rows([stage(passes[0], outs[0], None)] + rest)

    outs = [dict(ssm=[], conv=[], ffn=[]) for _ in passes]
    k_all = v_all = lf_pad = None
    cums = [None] * len(passes)
    for l in range(depth):
        shift, scale, gate = tables(2 * l)
        h = _norm_mod(x, scale, shift, bounds)
        if l < n_a:
            w_in = p["m_w_in"]
            zx = _matmul(h, w_in, layer=l, n=d_inner + conv_dim, bn=512)
            dt_raw = _matmul(h, w_in, layer=l, n_off=d_inner + conv_dim, n=nheads, bn=nheads)
            def mixer(ps, out, append):
                zxv, dtv = ps["view"](zx), ps["view"](dt_raw)
                xbc, tail = _causal_conv(zxv, ps["conv0"][l], p["m_conv_w"][l], p["m_conv_b"][l],
                                         boff=ps["boff"], t=ps["t"], col_off=d_inner, cdim=conv_dim,
                                         tc=2048, tt=256, out_dtype=F32)
                out["conv"].append(tail[:, SUBLANES - (p["m_conv_w"].shape[1] - 1):])
                y, sst = _ssd(xbc, zxv, dtv, p["m_dt_bias"][l], p["m_a_log"][l], p["m_d"][l],
                              p["m_norm_w"][l], ps["ssm0"][l], boff=ps["boff"], d_inner=d_inner,
                              ngroups=ngroups, nstate=nstate, chunk=min(SSD_CHUNK, ps["t"]),
                              append=append)
                out["ssm"].append(sst)
                return y

            y_all = stacked(mixer, _ssd_rows_per_step(passes[0]["t"], min(SSD_CHUNK, passes[0]["t"])))
            x = _matmul(y_all, p["m_w_out"], layer=l, bm=MM_ROWS_WIDE_K, bn=256, bk=d_inner,
                        epilogue="resid", res=x, gate_tbl=gate, bounds=bounds)
        else:
            j = l - n_a
            q = _matmul(h, p["w_q"], layer=j, out_dtype=MXU_DTYPE, bn=512, out_scale=dh ** -0.5 * LOG2E)
            os_ = []
            for ps, cum in zip(passes, cums):
                if ps["prefill"]:
                    qat, ka, vt = _fox_prep(ps["view"](q), ps["view"](k_all), ps["view"](v_all), cum[0],
                                            t=ps["t"], dh=dh)
                    o = _fox_prefill(qat, ka, vt, dh=dh)
                else:
                    o = _fox_decode(ps["view"](q), ps["k_past"], ps["v_past"], ps["view"](k_all),
                                    ps["view"](v_all), cum[0], cum[1], boff=ps["boff"], t=ps["t"], dh=dh)
                os_.append(o)
            x = _matmul(stack_rows(os_), p["w_o"], layer=j, bm=MM_ROWS_WIDE_K, bn=512,
                        epilogue="resid", res=x, gate_tbl=gate, bounds=bounds)
        shift, scale, gate = tables(2 * l + 1)
        h = _norm_mod(x, scale, shift, bounds)
        up = _matmul(h, p["f_w_up"], layer=l, bn=512)
        ffn_tt = 64

        def ffn_gate(ps, out, append):
            u, tail = _causal_conv(ps["view"](up), ps["ffn0"][l], p["f_conv_w"][l], p["f_conv_b"][l],
                                   boff=ps["boff"], t=ps["t"], col_off=d_ff, cdim=d_ff, tc=d_ff, tt=ffn_tt,
                                   out_dtype=MXU_DTYPE, gate_col_off=0, append=append)
            out["ffn"].append(tail[:, SUBLANES - (p["f_conv_w"].shape[1] - 1):])
            return u

        u_all = stacked(ffn_gate, _tile(passes[0]["t"], ffn_tt, SUBLANES))
        x = _matmul(u_all, p["f_w_down_lp"], layer=l, bm=MM_ROWS_WIDE_K, bn=256, bk=d_ff,
                    epilogue="resid", res=x, gate_tbl=gate, bounds=bounds)
        if l == n_a - 1:
            hkv = _norm_w(x[None], p["kv_norm_w"], boff=0, nb=1, t=r, out_dtype=MXU_DTYPE)[0]
            k_all = _matmul(hkv, p["w_kv"], n=d, bn=512)
            v_all = _matmul(hkv, p["w_kv"], n_off=d, n=d, bn=512)
            w_fg = jnp.pad(p["w_fgate"], ((0, 0), (0, LANES - nh_fox)))
            b_fg = jnp.pad(p["b_fgate"].astype(F32), (0, LANES - nh_fox)).reshape(1, LANES)
            lf_pad = _matmul(hkv, w_fg, bn=LANES, epilogue="logsig", bias=b_fg)
            tb = 256
            for i, ps in enumerate(passes):
                if ps["prefill"]:
                    assert ps["nb"] == 1 and ps["boff"] == 0 and ps["t"] % tb == 0
                    cums[i] = _cumsum_time(lf_pad[None], t=ps["t"], tb=tb)
                else:
                    lo = ps["boff"] * ps["t"]
                    lf_new = lf_pad[lo:lo + ps["nb"] * ps["t"]].reshape(ps["nb"], ps["t"], LANES)
                    lf_p = jnp.pad(ps["lf_past"].astype(F32), ((0, 0), (0, 0), (0, LANES - nh_fox)))
                    lf_all = jnp.concatenate([lf_p, lf_new], axis=1)
                    tot = lf_all.shape[1]
                    lf_all = jnp.pad(lf_all, ((0, 0), (0, (-tot) % tb), (0, 0)))
                    cums[i] = _cumsum_time(lf_all, t=lf_all.shape[1], tb=tb)
    results = []
    for ps, out in zip(passes, outs):
        lo, n = ps["boff"] * ps["t"], ps["nb"] * ps["t"]
        y_out = _norm_w(ps["view"](x), p["final_norm_w"], boff=ps["boff"], nb=ps["nb"], t=ps["t"],
                        out_dtype=x.dtype)
        seq = lambda a, w: a[lo:lo + n, :w].reshape((ps["nb"], ps["t"]) + ((nh_fox, dh) if w == d else (w,)))
        results.append((y_out, jnp.stack(out["ssm"]).astype(x.dtype), jnp.stack(out["conv"]),
                        jnp.stack(out["ffn"]), seq(k_all, d), seq(v_all, d),
                        seq(lf_pad, nh_fox).astype(x.dtype)))
    return results


def kernel(x_prompt, x_sample, c_prompt, c_sample, cache_k, cache_v, cache_logf, state_ssm, state_conv, state_ffn_conv, w_ada, b_ada, m_w_in, m_conv_w, m_conv_b, m_dt_bias, m_a_log, m_d, m_norm_w, m_w_out, kv_norm_w, w_kv, w_fgate, b_fgate, w_q, w_o, f_w_up, f_conv_w, f_conv_b, f_w_down, final_norm_w):
    p = dict(w_ada=w_ada, b_ada=b_ada, m_w_in=m_w_in, m_conv_w=m_conv_w, m_conv_b=m_conv_b,
             m_dt_bias=m_dt_bias, m_a_log=m_a_log, m_d=m_d, m_norm_w=m_norm_w, m_w_out=m_w_out,
             kv_norm_w=kv_norm_w, w_kv=w_kv, w_fgate=w_fgate, b_fgate=b_fgate, w_q=w_q, w_o=w_o,
             f_w_up=f_w_up, f_conv_w=f_conv_w, f_conv_b=f_conv_b, f_w_down=f_w_down,
             final_norm_w=final_norm_w)
    p["f_w_down_lp"] = f_w_down.astype(MXU_DTYPE)
    bp, tp, d = x_prompt.shape
    bs, ts, _ = x_sample.shape
    assert bp == 1 and tp % ts == 0 and ts % PACKED_ROWS == 0
    n_a, depth = m_w_in.shape[0], w_ada.shape[0]
    nheads, hd, nstate = state_ssm.shape[2], state_ssm.shape[3], state_ssm.shape[4]
    past = cache_k.shape[1]
    dtp = x_prompt.dtype
    r = tp + bs * ts

    nseq = bp + bs
    assert nseq <= SEQ_TABLE_ROWS
    c_rows = jnp.pad(jnp.concatenate([c_prompt, c_sample], axis=0), ((0, SEQ_TABLE_ROWS - nseq), (0, 0)))
    mod = _ada_all(c_rows, w_ada, b_ada)

    x = jnp.concatenate([x_prompt.reshape(tp, d), x_sample.reshape(bs * ts, d)], axis=0)
    bounds = ((0, tp, 0),) + tuple((tp + b * ts, tp + (b + 1) * ts, 1 + b) for b in range(bs))
    passes = [
        dict(prefill=True, boff=0, nb=1, t=tp, view=lambda a: a.reshape(1, r, a.shape[-1]),
             ssm0=jnp.zeros((n_a, bp, nheads, hd, nstate), F32),
             conv0=jnp.zeros((n_a, bp, m_conv_w.shape[1] - 1, m_conv_w.shape[2]), dtp),
             ffn0=jnp.zeros((depth, bp, f_conv_w.shape[1] - 1, f_conv_w.shape[2]), dtp)),
        dict(prefill=False, boff=tp // ts, nb=bs, t=ts, view=lambda a: a.reshape(r // ts, ts, a.shape[-1]),
             ssm0=state_ssm, conv0=state_conv, ffn0=state_ffn_conv, lf_past=cache_logf,
             k_past=cache_k.reshape(bs, past, d), v_past=cache_v.reshape(bs, past, d)),
    ]
    out_p, out_s = _forward(x, mod, passes, bounds, cache_k, cache_v, cache_logf, p)
    return (out_p[0], out_s[0]) + out_p[1:] + out_s[1:]
```

```python
import functools
import math

import jax
import jax.numpy as jnp
from jax import lax
from jax.experimental import pallas as pl
from jax.experimental.pallas import tpu as pltpu

F32 = jnp.float32
MXU_DTYPE = jnp.bfloat16
EPS = 1e-6
LANES = 128
SUBLANES = 8
PACKED_ROWS = 16
VMEM_LIMIT_BYTES = 56 * 1024 * 1024
SSD_CHUNK = 64
SSD_CHUNKS_PER_STEP = 2
NEG_INF = float("-inf")
LOG2E = math.log2(math.e)
MM_ROWS = 2080
MM_ROWS_WIDE_K = 1040
NORM_ROWS = 320
SEQ_TABLE_ROWS = 16


def _tile(dim, pref, align):
    t = (min(pref, dim) // align) * align
    while t >= align:
        if dim % t == 0:
            return t
        t -= align
    return dim


def _params(*sem):
    return pltpu.CompilerParams(dimension_semantics=sem, vmem_limit_bytes=VMEM_LIMIT_BYTES)


def _silu(x):
    return x * jax.nn.sigmoid(x)


def _softplus(x):
    return jnp.maximum(x, 0.0) + jnp.log1p(jnp.exp(-jnp.abs(x)))


def _split3(a):
    a1 = a.astype(MXU_DTYPE)
    r1 = a - a1.astype(F32)
    a2 = r1.astype(MXU_DTYPE)
    r2 = r1 - a2.astype(F32)
    return a1, a2, r2.astype(MXU_DTYPE)


def _dot01(m01, a, dims=(((1,), (0,)), ((), ()))):
    m = m01.astype(MXU_DTYPE)
    out = None
    for t in _split3(a):
        part = lax.dot_general(m, t, dims, preferred_element_type=F32)
        out = part if out is None else out + part
    return out


def _dot01r(a, m01):
    out = None
    for t in _split3(a):
        part = jnp.dot(t, m01.astype(MXU_DTYPE), preferred_element_type=F32)
        out = part if out is None else out + part
    return out


def _tile_segments(rows_per_tile, ntiles, bounds):
    out = []
    for i in range(ntiles):
        lo, hi = i * rows_per_tile, (i + 1) * rows_per_tile
        out.append(tuple((max(s, lo) - lo, min(e, hi) - lo, q) for s, e, q in bounds
                         if max(s, lo) < min(e, hi)))
    return tuple(out)


def _for_tile_segments(i, tile_segs, emit):
    groups = {}
    for t, segs in enumerate(tile_segs):
        groups.setdefault(segs, []).append(t)
    if len(groups) == 1:
        emit(tile_segs[0])
        return
    for segs, tiles in groups.items():
        runs = []
        for t in tiles:
            if runs and runs[-1][1] == t - 1:
                runs[-1][1] = t
            else:
                runs.append([t, t])
        cond = None
        for a, b in runs:
            c = (i == a) if a == b else ((i >= a) & (i <= b))
            cond = c if cond is None else (cond | c)
        pl.when(cond)(functools.partial(emit, segs))


def _ada_kernel(c_ref, w_ref, b_ref, o_ref):
    a = _silu(c_ref[...]).astype(MXU_DTYPE)
    w = w_ref[...].astype(MXU_DTYPE)
    o_ref[...] = jnp.dot(a, w, preferred_element_type=F32) + b_ref[...]


def _ada_all(c_rows, w_ada, b_ada):
    r, d = c_rows.shape
    s = w_ada.shape[0] * w_ada.shape[1]
    n = w_ada.shape[-1]
    w = w_ada.reshape(s, d, n)
    b = b_ada.reshape(s, 1, n)
    tn = _tile(n, 512, LANES)
    return pl.pallas_call(
        _ada_kernel,
        out_shape=jax.ShapeDtypeStruct((s, r, n), F32),
        grid=(s, n // tn),
        in_specs=[
            pl.BlockSpec((r, d), lambda i, j: (0, 0)),
            pl.BlockSpec((None, d, tn), lambda i, j: (i, 0, j)),
            pl.BlockSpec((None, 1, tn), lambda i, j: (i, 0, j)),
        ],
        out_specs=pl.BlockSpec((None, r, tn), lambda i, j: (i, 0, j)),
        compiler_params=_params("parallel", "parallel"),
        name="ada_mod",
    )(c_rows, w, b)


def _rms(x):
    return x * lax.rsqrt(jnp.mean(x * x, axis=-1, keepdims=True) + EPS)


def _norm_mod_kernel(x_ref, scale_ref, shift_ref, o_ref, *, tile_segs):
    y = _rms(x_ref[...])

    def emit(segs):
        for lo, hi, s in segs:
            v = y[lo:hi, :] * (1.0 + scale_ref[s:s + 1, :]) + shift_ref[s:s + 1, :]
            o_ref[lo:hi, :] = v.astype(o_ref.dtype)

    _for_tile_segments(pl.program_id(0), tile_segs, emit)


def _norm_mod(x, scale_tbl, shift_tbl, bounds):
    r, d = x.shape
    tt = _tile(r, NORM_ROWS, PACKED_ROWS)
    nt = r // tt
    s = scale_tbl.shape[0]
    return pl.pallas_call(
        functools.partial(_norm_mod_kernel, tile_segs=_tile_segments(tt, nt, bounds)),
        out_shape=jax.ShapeDtypeStruct((r, d), MXU_DTYPE),
        grid=(nt,),
        in_specs=[pl.BlockSpec((tt, d), lambda i: (i, 0)),
                  pl.BlockSpec((s, d), lambda i: (0, 0)),
                  pl.BlockSpec((s, d), lambda i: (0, 0))],
        out_specs=pl.BlockSpec((tt, d), lambda i: (i, 0)),
        compiler_params=_params("parallel"),
        name="rmsnorm_mod",
    )(x, scale_tbl, shift_tbl)


def _norm_w_kernel(x_ref, w_ref, o_ref):
    o_ref[...] = (_rms(x_ref[...]) * w_ref[...]).astype(o_ref.dtype)


def _norm_w(xv, w, *, boff, nb, t, out_dtype):
    d = xv.shape[-1]
    tt = _tile(t, NORM_ROWS, PACKED_ROWS)
    return pl.pallas_call(
        _norm_w_kernel,
        out_shape=jax.ShapeDtypeStruct((nb, t, d), out_dtype),
        grid=(nb, t // tt),
        in_specs=[pl.BlockSpec((None, tt, d), lambda i, j: (i + boff, j, 0)),
                  pl.BlockSpec((1, d), lambda i, j: (0, 0))],
        out_specs=pl.BlockSpec((None, tt, d), lambda i, j: (i, j, 0)),
        compiler_params=_params("parallel", "parallel"),
        name="rmsnorm_w",
    )(xv, w.astype(F32).reshape(1, d))


def _mm_kernel(*refs, nk, epilogue, out_scale, tile_segs):
    x_ref, w_ref = refs[0], refs[1]
    pos = 2
    if epilogue == "resid":
        res_ref, gate_ref = refs[2], refs[3]
        pos = 4
    elif epilogue == "logsig":
        bias_ref = refs[2]
        pos = 3
    o_ref = refs[pos]
    part = jnp.dot(x_ref[...], w_ref[...].astype(MXU_DTYPE), preferred_element_type=F32)

    def finish(acc):
        if out_scale is not None:
            acc = acc * out_scale
        if epilogue == "resid":
            def emit(segs):
                for lo, hi, s in segs:
                    v = res_ref[lo:hi, :] + gate_ref[s:s + 1, :] * acc[lo:hi, :]
                    o_ref[lo:hi, :] = v.astype(o_ref.dtype)

            _for_tile_segments(pl.program_id(0), tile_segs, emit)
            return
        if epilogue == "logsig":
            acc = -_softplus(-(acc + bias_ref[...]))
        o_ref[...] = acc.astype(o_ref.dtype)

    if nk == 1:
        finish(part)
    else:
        acc_ref = refs[pos + 1]
        k = pl.program_id(2)

        @pl.when(k == 0)
        def _():
            acc_ref[...] = part

        @pl.when(k > 0)
        def _():
            acc_ref[...] += part

        @pl.when(k == nk - 1)
        def _():
            finish(acc_ref[...])


def _matmul(x, w, *, layer=0, n_off=0, n=None, out_dtype=F32, bm=None, bn=512, bk=4096,
            epilogue=None, res=None, gate_tbl=None, bounds=None, bias=None, out_scale=None):
    m, kdim = x.shape
    if w.ndim == 2:
        w = w[None]
    n = w.shape[2] if n is None else n
    bm = _tile(m, MM_ROWS if bm is None else bm, PACKED_ROWS)
    bn = _tile(math.gcd(n, n_off), bn, LANES)
    bk = _tile(kdim, bk, LANES)
    joff = n_off // bn
    nk = kdim // bk
    ins = [x, w]
    x_mode = dict(pipeline_mode=pl.Buffered(1)) if nk == 1 and n // bn > 1 else {}
    specs = [pl.BlockSpec((bm, bk), lambda i, j, k: (i, k), **x_mode),
             pl.BlockSpec((None, bk, bn), lambda i, j, k: (layer, k, j + joff))]
    tile_segs = None
    if epilogue == "resid":
        ins += [res, gate_tbl]
        specs.append(pl.BlockSpec((bm, bn), lambda i, j, k: (i, j)))
        specs.append(pl.BlockSpec((gate_tbl.shape[0], bn), lambda i, j, k: (0, j)))
        tile_segs = _tile_segments(bm, m // bm, bounds)
    elif epilogue == "logsig":
        ins.append(bias)
        specs.append(pl.BlockSpec((1, bn), lambda i, j, k: (0, j)))
    scratch = [pltpu.VMEM((bm, bn), F32)] if nk > 1 else []
    return pl.pallas_call(
        functools.partial(_mm_kernel, nk=nk, epilogue=epilogue, out_scale=out_scale,
                          tile_segs=tile_segs),
        out_shape=jax.ShapeDtypeStruct((m, n), out_dtype),
        grid=(m // bm, n // bn, nk),
        in_specs=specs,
        out_specs=pl.BlockSpec((bm, bn), lambda i, j, k: (i, j)),
        scratch_shapes=scratch,
        compiler_params=_params("parallel", "parallel", "arbitrary"),
        name="matmul_" + (epilogue or "plain"),
    )(*ins)


def _conv_kernel(*refs, width, tt, gated, nt, appending):
    refs = list(refs)
    a_ref = refs.pop(0) if gated else None
    x_ref, buf_ref, w_ref, b_ref = refs[:4]
    more_ref = refs[4] if appending else None
    y_ref, tail_ref, xx_ref = refs[-3:]
    ti = pl.program_id(2)

    @pl.when(ti == 0)
    def _():
        xx_ref[0:SUBLANES, :] = buf_ref[...]

    @pl.when(ti < nt)
    def _():
        xx_ref[SUBLANES:SUBLANES + tt, :] = x_ref[...]
        acc = b_ref[...]
        for k in range(width):
            lo = SUBLANES - (width - 1) + k
            acc = acc + w_ref[k:k + 1, :] * xx_ref[lo:lo + tt, :]
        y = _silu(acc)
        if gated:
            y = y * a_ref[...]
        y_ref[...] = y.astype(y_ref.dtype)
        tail = xx_ref[tt:tt + SUBLANES, :]
        xx_ref[0:SUBLANES, :] = tail
        tail_ref[...] = tail

    if appending:
        @pl.when(ti >= nt)
        def _():
            y_ref[...] = more_ref[...]


def _causal_conv(srcv, buf, w, b, *, boff, t, col_off, cdim, tc, tt, out_dtype, gate_col_off=None,
                 append=None):
    nb = buf.shape[0]
    width = w.shape[0]
    assert t >= SUBLANES and width - 1 <= SUBLANES
    tt = _tile(t, tt, SUBLANES)
    tc = _tile(math.gcd(cdim, col_off, gate_col_off or 0), tc, LANES)
    coff = col_off // tc
    nt = t // tt
    extra = 0 if append is None else append.shape[0]
    assert extra % tt == 0 and (extra == 0 or nb == 1)
    last = lambda k: jnp.minimum(k, nt - 1)
    buf8 = jnp.pad(buf.astype(F32), ((0, 0), (SUBLANES - (width - 1), 0), (0, 0)))
    w8 = jnp.pad(w, ((0, SUBLANES - width), (0, 0)))
    b2 = b.reshape(1, cdim)
    gated = gate_col_off is not None
    ins, specs = [], []
    if gated:
        goff = gate_col_off // tc
        ins.append(srcv)
        specs.append(pl.BlockSpec((None, tt, tc), lambda i, j, k: (i + boff, last(k), j + goff)))
    ins += [srcv, buf8, w8, b2]
    specs += [
        pl.BlockSpec((None, tt, tc), lambda i, j, k: (i + boff, last(k), j + coff)),
        pl.BlockSpec((None, SUBLANES, tc), lambda i, j, k: (i, 0, j)),
        pl.BlockSpec((SUBLANES, tc), lambda i, j, k: (0, j)),
        pl.BlockSpec((1, tc), lambda i, j, k: (0, j)),
    ]
    if extra:
        ins.append(append)
        specs.append(pl.BlockSpec((tt, tc), lambda i, j, k: (jnp.maximum(k - nt, 0), j)))
    return pl.pallas_call(
        functools.partial(_conv_kernel, width=width, tt=tt, gated=gated, nt=nt, appending=extra > 0),
        out_shape=(jax.ShapeDtypeStruct((nb, t + extra, cdim), out_dtype),
                   jax.ShapeDtypeStruct((nb, SUBLANES, cdim), F32)),
        grid=(nb, cdim // tc, nt + extra // tt),
        in_specs=specs,
        out_specs=(pl.BlockSpec((None, tt, tc), lambda i, j, k: (i, k, j)),
                   pl.BlockSpec((None, SUBLANES, tc), lambda i, j, k: (i, 0, j))),
        scratch_shapes=[pltpu.VMEM((tt + SUBLANES, tc), F32)],
        compiler_params=_params("parallel", "parallel", "arbitrary"),
        name="causal_conv_gated" if gated else "causal_conv",
    )(*ins)


def _ssd_kernel(*refs, nc, appending, **kw):
    if not appending:
        _ssd_chunk(*refs, nc=nc, **kw)
        return
    more_ref, y_ref = refs[11], refs[12]
    c = pl.program_id(2)

    @pl.when(c < nc)
    def _():
        _ssd_chunk(*(refs[:11] + refs[12:]), nc=nc, **kw)

    @pl.when(c >= nc)
    def _():
        y_ref[...] = more_ref[...]


def _ssd_chunk(x_ref, b_ref, c_ref, z_ref, dt_ref, dtb_ref, alog_ref, dsk_ref, nw_ref, e_ref, h0_ref,
               y_ref, hout_ref, ht_ref, yg_ref, *, hpg, hd, ln, nheads, nc, cps):
    g = pl.program_id(1)
    c = pl.program_id(2)
    gw = hpg * hd
    per_blk = LANES // hd
    nblk = gw // LANES
    nblk_p = max(nblk, SUBLANES)
    nt = (((1,), (1,)), ((), ()))

    @pl.when(c == 0)
    def _():
        ht_ref[...] = h0_ref[...].T

    a_neg = -jnp.exp(alog_ref[...])
    shift = lax.rem(nheads - g * hpg, nheads)
    dsk_g = pltpu.roll(jnp.broadcast_to(dsk_ref[...], (SUBLANES, nheads)), shift, axis=1)
    expand = e_ref[...]

    tril = (lax.broadcasted_iota(jnp.int32, (ln, ln), 0) >= lax.broadcasted_iota(jnp.int32, (ln, ln), 1))
    rowi = lax.broadcasted_iota(jnp.int32, (ln, LANES), 0)
    lanei = lax.broadcasted_iota(jnp.int32, (ln, LANES), 1)
    in_slot = lanei & (hd - 1)
    slot_causal = (in_slot <= rowi) & (in_slot < ln)
    slot_of_lane = [(lanei >= w * hd) & (lanei < (w + 1) * hd) for w in range(per_blk)]
    sel_blk = [lax.broadcasted_iota(jnp.int32, (nblk_p, nheads), 1)
               == per_blk * lax.broadcasted_iota(jnp.int32, (nblk_p, nheads), 0) + w
               for w in range(per_blk)]

    def in_slots(parts):
        rows = []
        for part in parts:
            rows.append(part)
            if hd > ln:
                rows.append(jnp.zeros((hd - ln, part.shape[1]), part.dtype))
        return jnp.concatenate(rows, axis=0)

    for k in range(cps):
        r0 = k * ln
        dt = _softplus(dt_ref[r0:r0 + ln, :] + dtb_ref[...])
        dt_g = pltpu.roll(dt, shift, axis=1)
        da_g = pltpu.roll(dt * a_neg, shift, axis=1)
        a_cum = _dot01(tril, da_g)
        ex = _dot01r(jnp.concatenate([a_cum, dt_g, dsk_g], axis=0), expand)
        acol_all, dcol_all, dsk_all = ex[0:ln], ex[ln:2 * ln], ex[2 * ln:2 * ln + 1]
        a_rows = None
        zero_rows = jnp.zeros((ln, nheads), F32)
        for w in range(per_blk):
            placed = in_slots([a_cum if v == w else zero_rows for v in range(per_blk)])
            part = _dot01(sel_blk[w], placed, nt)
            a_rows = part if a_rows is None else a_rows + part

        bmat = b_ref[r0:r0 + ln, :].astype(MXU_DTYPE)
        cmat = c_ref[r0:r0 + ln, :].astype(MXU_DTYPE)
        cb = lax.dot_general(cmat, in_slots([bmat] * per_blk), nt,
                             preferred_element_type=F32)

        ysq = jnp.zeros((ln, LANES), F32)
        for j in range(nblk):
            lo, hi = j * LANES, (j + 1) * LANES
            x_blk = x_ref[r0:r0 + ln, lo:hi]
            acol, dcol = acol_all[:, lo:hi], dcol_all[:, lo:hi]
            alast = acol[ln - 1:ln, :]
            xdt = x_blk * dcol
            xw_m = (xdt * jnp.exp(alast - acol)).astype(MXU_DTYPE)
            dec = jnp.exp(jnp.where(slot_causal, acol - a_rows[j:j + 1, :], NEG_INF))
            xbd = in_slots([jnp.where(slot_of_lane[w], xdt, 0.0) for w in range(per_blk)])
            xbd = xbd.astype(MXU_DTYPE)
            ydiag = jnp.dot((cb * dec).astype(MXU_DTYPE), xbd, preferred_element_type=F32)
            h_blk = ht_ref[:, lo:hi]
            yoff = jnp.dot(cmat, h_blk.astype(MXU_DTYPE), preferred_element_type=F32) * jnp.exp(acol)
            st = lax.dot_general(bmat, xw_m, (((0,), (0,)), ((), ())), preferred_element_type=F32)
            ht_ref[:, lo:hi] = h_blk * jnp.exp(alast) + st
            y = ydiag + yoff + dsk_all[:, lo:hi] * x_blk
            y = y * _silu(z_ref[r0:r0 + ln, lo:hi])
            yg_ref[r0:r0 + ln, lo:hi] = y
            ysq = ysq + y * y

        inv = lax.rsqrt(jnp.sum(ysq, axis=-1, keepdims=True) * (1.0 / gw) + EPS)
        y_ref[r0:r0 + ln, :] = (yg_ref[r0:r0 + ln, :] * inv * nw_ref[...]).astype(y_ref.dtype)

    @pl.when(c == nc - 1)
    def _():
        hout_ref[...] = ht_ref[...].T


def _ssd_rows_per_step(t, chunk):
    cps = SSD_CHUNKS_PER_STEP if t % (SSD_CHUNKS_PER_STEP * chunk) == 0 else 1
    return cps * chunk


def _ssd(xbc, zxv, dtv, dt_bias, a_log, d_skip, norm_w, h0, *, boff, d_inner, ngroups, nstate, chunk,
         append=None):
    nb, t, _ = xbc.shape
    nheads = dtv.shape[-1]
    hd = d_inner // nheads
    hpg = nheads // ngroups
    gw = hpg * hd
    assert nheads <= LANES
    assert LANES % hd == 0 and gw % LANES == 0 and nstate % LANES == 0 and t % chunk == 0
    assert hd & (hd - 1) == 0 and chunk <= hd
    rows = _ssd_rows_per_step(t, chunk)
    cps = rows // chunk
    nc = t // rows
    h0g = h0.astype(F32).reshape(nb, ngroups, gw, nstate)
    vec = lambda a: a.astype(F32).reshape(1, nheads)
    xoff = d_inner // nstate
    extra = 0 if append is None else append.shape[0]
    assert extra % rows == 0 and (extra == 0 or nb == 1)
    last = lambda c: jnp.minimum(c, nc - 1)
    kern = functools.partial(_ssd_kernel, hpg=hpg, hd=hd, ln=chunk, nheads=nheads, nc=nc, cps=cps,
                             appending=extra > 0)
    expand = (jnp.arange(nheads)[:, None] == jnp.arange(gw)[None, :] // hd).astype(MXU_DTYPE)
    ins = [xbc, xbc, xbc, zxv, dtv, vec(dt_bias), vec(a_log), vec(d_skip),
           norm_w.astype(F32).reshape(1, d_inner), expand, h0g]
    specs = [
        pl.BlockSpec((None, rows, gw), lambda b, g, c: (b, last(c), g)),
        pl.BlockSpec((None, rows, nstate), lambda b, g, c: (b, last(c), xoff + g)),
        pl.BlockSpec((None, rows, nstate), lambda b, g, c: (b, last(c), xoff + ngroups + g)),
        pl.BlockSpec((None, rows, gw), lambda b, g, c: (b + boff, last(c), g)),
        pl.BlockSpec((None, rows, nheads), lambda b, g, c: (b + boff, last(c), 0)),
        pl.BlockSpec((1, nheads), lambda b, g, c: (0, 0)),
        pl.BlockSpec((1, nheads), lambda b, g, c: (0, 0)),
        pl.BlockSpec((1, nheads), lambda b, g, c: (0, 0)),
        pl.BlockSpec((1, gw), lambda b, g, c: (0, g)),
        pl.BlockSpec((nheads, gw), lambda b, g, c: (0, 0)),
        pl.BlockSpec((None, None, gw, nstate), lambda b, g, c: (b, g, 0, 0)),
    ]
    if extra:
        ins.append(append)
        specs.append(pl.BlockSpec((rows, gw), lambda b, g, c: (jnp.maximum(c - nc, 0), g)))
    y, hout = pl.pallas_call(
        kern,
        out_shape=(jax.ShapeDtypeStruct((nb, t + extra, d_inner), MXU_DTYPE),
                   jax.ShapeDtypeStruct((nb, ngroups, gw, nstate), F32)),
        grid=(nb, ngroups, nc + extra // rows),
        in_specs=specs,
        out_specs=(pl.BlockSpec((None, rows, gw), lambda b, g, c: (b, c, g)),
                   pl.BlockSpec((None, None, gw, nstate), lambda b, g, c: (b, g, 0, 0))),
        scratch_shapes=[pltpu.VMEM((nstate, gw), F32), pltpu.VMEM((rows, gw), F32)],
        compiler_params=_params("parallel", "parallel", "arbitrary"),
        name="ssd_scan",
    )(*ins)
    return y, hout.reshape(nb, nheads, hd, nstate)


def _cumsum_kernel(lf_ref, c_ref, ct_ref, carry_ref, *, tb):
    @pl.when(pl.program_id(1) == 0)
    def _():
        carry_ref[...] = jnp.zeros_like(carry_ref)

    row = lax.broadcasted_iota(jnp.int32, (tb, tb), 0)
    col = lax.broadcasted_iota(jnp.int32, (tb, tb), 1)
    cs = _dot01(row >= col, lf_ref[...]) + carry_ref[0:1, :]
    c_ref[...] = cs
    ct_ref[...] = cs.T
    carry_ref[...] = jnp.broadcast_to(cs[tb - 1:tb, :], carry_ref.shape)


def _cumsum_time(lfv, *, t, tb):
    bsz, _, w = lfv.shape
    return pl.pallas_call(
        functools.partial(_cumsum_kernel, tb=tb),
        out_shape=(jax.ShapeDtypeStruct((bsz, t, w), F32), jax.ShapeDtypeStruct((bsz, w, t), F32)),
        grid=(bsz, t // tb),
        in_specs=[pl.BlockSpec((None, tb, w), lambda b, i: (b, i, 0))],
        out_specs=(pl.BlockSpec((None, tb, w), lambda b, i: (b, i, 0)),
                   pl.BlockSpec((None, w, tb), lambda b, i: (b, 0, i))),
        scratch_shapes=[pltpu.VMEM((SUBLANES, w), F32)],
        compiler_params=_params("parallel", "arbitrary"),
        name="logf_cumsum",
    )(lfv)


def _head_column(c_blk, h, width):
    rolled = pltpu.roll(c_blk, lax.rem(LANES - h, LANES), axis=1)
    return jnp.broadcast_to(rolled[:, 0:1], (c_blk.shape[0], width))


def _aug_cols(c_blk, h, for_query):
    col = _head_column(c_blk, h, LANES) * LOG2E
    c1, c2, c3 = (t.astype(F32) for t in _split3(col))
    one = jnp.ones_like(col)
    terms = (c1, c2, c3, one, one, one) if for_query else (one, one, one, -c1, -c2, -c3)
    lane = lax.broadcasted_iota(jnp.int32, col.shape, 1)
    out = jnp.zeros_like(col)
    for idx, term in enumerate(terms):
        out = jnp.where(lane == idx, term, out)
    return out


def _fox_prep_kernel(q_ref, k_ref, v_ref, c_ref, qat_ref, ka_ref, vt_ref):
    h = pl.program_id(1)
    dh = q_ref.shape[-1]
    c_blk = c_ref[...]
    ka_ref[:, 0:dh] = k_ref[...].astype(MXU_DTYPE)
    ka_ref[:, dh:dh + LANES] = _aug_cols(c_blk, h, False).astype(MXU_DTYPE)
    qat_ref[0:dh, :] = q_ref[...].astype(F32).T.astype(MXU_DTYPE)
    qat_ref[dh:dh + LANES, :] = _aug_cols(c_blk, h, True).T.astype(MXU_DTYPE)
    vt_ref[...] = v_ref[...].T.astype(MXU_DTYPE)


def _fox_prep(qv, kv, vv, c, *, t, dh):
    bsz, _, d = qv.shape
    nh = d // dh
    tt = _tile(t, 1024, LANES)
    ka_w = dh + LANES
    return pl.pallas_call(
        _fox_prep_kernel,
        out_shape=(jax.ShapeDtypeStruct((bsz, nh, ka_w, t), MXU_DTYPE),
                   jax.ShapeDtypeStruct((bsz, nh, t, ka_w), MXU_DTYPE),
                   jax.ShapeDtypeStruct((bsz, nh, dh, t), MXU_DTYPE)),
        grid=(bsz, nh, t // tt),
        in_specs=[
            pl.BlockSpec((None, tt, dh), lambda b, h, i: (b, i, h)),
            pl.BlockSpec((None, tt, dh), lambda b, h, i: (b, i, h)),
            pl.BlockSpec((None, tt, dh), lambda b, h, i: (b, i, h)),
            pl.BlockSpec((None, tt, LANES), lambda b, h, i: (b, i, 0)),
        ],
        out_specs=(pl.BlockSpec((None, None, ka_w, tt), lambda b, h, i: (b, h, 0, i)),
                   pl.BlockSpec((None, None, tt, ka_w), lambda b, h, i: (b, h, i, 0)),
                   pl.BlockSpec((None, None, dh, tt), lambda b, h, i: (b, h, 0, i))),
        compiler_params=_params("parallel", "parallel", "parallel"),
        name="fox_prep",
    )(qv, kv, vv, c)


def _fox_prefill_kernel(qi_tbl, ki_tbl, qat_ref, ka_ref, vt_ref, o_ref, m_ref, l_ref, acc_ref, *, tq):
    step_id = pl.program_id(2)
    qi = qi_tbl[step_id]
    ki = ki_tbl[step_id]

    @pl.when(ki == 0)
    def _():
        m_ref[...] = jnp.full_like(m_ref, NEG_INF)
        l_ref[...] = jnp.zeros_like(l_ref)
        acc_ref[...] = jnp.zeros_like(acc_ref)

    def step(diagonal):
        s = jnp.dot(ka_ref[...], qat_ref[...], preferred_element_type=F32)
        if diagonal:
            krow = lax.broadcasted_iota(jnp.int32, (tq, tq), 0)
            qcol = lax.broadcasted_iota(jnp.int32, (tq, tq), 1)
            s = jnp.where(krow <= qcol, s, NEG_INF)
        m_prev = m_ref[...]
        m_new = jnp.maximum(m_prev, jnp.max(s, axis=0, keepdims=True))
        p = jnp.exp2(s - m_new)
        alpha = jnp.exp2(m_prev - m_new)
        l_ref[...] = alpha * l_ref[...] + jnp.sum(p, axis=0, keepdims=True)
        acc_ref[...] = alpha * acc_ref[...] + jnp.dot(
            vt_ref[...], p.astype(MXU_DTYPE), preferred_element_type=F32)
        m_ref[...] = m_new

    @pl.when(ki < qi)
    def _():
        step(False)

    @pl.when(ki == qi)
    def _():
        step(True)
        o_ref[...] = (acc_ref[...] / l_ref[...]).T.astype(o_ref.dtype)


def _fox_prefill(qat, ka, vt, *, dh):
    bsz, nh, ka_w, t = qat.shape
    tq = _tile(t, 1024, LANES)
    nq = t // tq
    pairs = [(qi, ki) for qi in range(nq) for ki in range(qi + 1)]
    qi_tbl = jnp.asarray([pr[0] for pr in pairs], jnp.int32)
    ki_tbl = jnp.asarray([pr[1] for pr in pairs], jnp.int32)
    grid_spec = pltpu.PrefetchScalarGridSpec(
        num_scalar_prefetch=2,
        grid=(bsz, nh, len(pairs)),
        in_specs=[
            pl.BlockSpec((None, None, ka_w, tq), lambda b, h, s, qt, kt: (b, h, 0, qt[s])),
            pl.BlockSpec((None, None, tq, ka_w), lambda b, h, s, qt, kt: (b, h, kt[s], 0)),
            pl.BlockSpec((None, None, dh, tq), lambda b, h, s, qt, kt: (b, h, 0, kt[s])),
        ],
        out_specs=pl.BlockSpec((None, tq, dh), lambda b, h, s, qt, kt: (b, qt[s], h)),
        scratch_shapes=[pltpu.VMEM((1, tq), F32), pltpu.VMEM((1, tq), F32), pltpu.VMEM((dh, tq), F32)],
    )
    return pl.pallas_call(
        functools.partial(_fox_prefill_kernel, tq=tq),
        out_shape=jax.ShapeDtypeStruct((bsz, t, nh * dh), MXU_DTYPE),
        grid_spec=grid_spec,
        compiler_params=_params("parallel", "parallel", "arbitrary"),
        name="fox_prefill",
    )(qi_tbl, ki_tbl, qat, ka, vt)


def _fox_decode_kernel(q_ref, kp_ref, vp_ref, kn_ref, vn_ref, cq_ref, ckt_ref, o_ref, *, past, tq, dh):
    hg = pl.program_id(1)
    nt = (((1,), (1,)), ((), ()))
    row = lax.broadcasted_iota(jnp.int32, (tq, tq), 0)
    col = lax.broadcasted_iota(jnp.int32, (tq, tq), 1)
    cq_blk = cq_ref[...]
    for hh in range(SUBLANES):
        h = hg * SUBLANES + hh
        lo, hi = hh * dh, (hh + 1) * dh
        q = q_ref[:, lo:hi]
        ck = ckt_ref[pl.ds(h, 1), :]
        cq_p = _head_column(cq_blk, h, past)
        s_p = lax.dot_general(q, kp_ref[:, hh, :].astype(MXU_DTYPE), nt, preferred_element_type=F32)
        s_p = s_p + (cq_p - ck[:, 0:past]) * LOG2E
        s_n = lax.dot_general(q, kn_ref[:, lo:hi].astype(MXU_DTYPE), nt, preferred_element_type=F32)
        s_n = s_n + (cq_p[:, 0:tq] - ck[:, past:past + tq]) * LOG2E
        s_n = jnp.where(col <= row, s_n, NEG_INF)
        m = jnp.maximum(jnp.max(s_p, axis=-1, keepdims=True), jnp.max(s_n, axis=-1, keepdims=True))
        p_p = jnp.exp2(s_p - m)
        p_n = jnp.exp2(s_n - m)
        l = jnp.sum(p_p, axis=-1, keepdims=True) + jnp.sum(p_n, axis=-1, keepdims=True)
        acc = jnp.dot(p_p.astype(MXU_DTYPE), vp_ref[:, hh, :].astype(MXU_DTYPE), preferred_element_type=F32)
        acc = acc + jnp.dot(p_n.astype(MXU_DTYPE), vn_ref[:, lo:hi].astype(MXU_DTYPE),
                            preferred_element_type=F32)
        o_ref[:, lo:hi] = (acc / l).astype(o_ref.dtype)


def _fox_decode(qv, k_past, v_past, knv, vnv, c, ct, *, boff, t, dh):
    nb, past, nh, _ = k_past.shape
    d = nh * dh
    tpad = ct.shape[-1]
    assert past % t == 0 and past % LANES == 0 and nh % SUBLANES == 0
    hgroups = nh // SUBLANES
    gwid = SUBLANES * dh
    tiled = lambda a: a.reshape(nb, past, hgroups, SUBLANES, dh)
    new_blk = lambda b, hg: (b + boff, 0, hg)
    past_blk = pl.BlockSpec((None, past, None, SUBLANES, dh), lambda b, hg: (b, 0, hg, 0, 0))
    return pl.pallas_call(
        functools.partial(_fox_decode_kernel, past=past, tq=t, dh=dh),
        out_shape=jax.ShapeDtypeStruct((nb, t, d), MXU_DTYPE),
        grid=(nb, hgroups),
        in_specs=[
            pl.BlockSpec((None, t, gwid), new_blk),
            past_blk,
            past_blk,
            pl.BlockSpec((None, t, gwid), new_blk),
            pl.BlockSpec((None, t, gwid), new_blk),
            pl.BlockSpec((None, t, LANES), lambda b, hg: (b, past // t, 0)),
            pl.BlockSpec((None, LANES, tpad), lambda b, hg: (b, 0, 0)),
        ],
        out_specs=pl.BlockSpec((None, t, gwid), lambda b, hg: (b, 0, hg)),
        compiler_params=_params("parallel", "parallel"),
        name="fox_decode",
    )(qv, tiled(k_past), tiled(v_past), knv, vnv, c, ct)


def _forward(x, mod, passes, bounds, k_past, v_past, lf_past, p):
    r, d = x.shape
    depth = p["w_ada"].shape[0]
    n_a = p["m_w_in"].shape[0]
    d_inner = p["m_w_out"].shape[1]
    conv_dim = p["m_conv_w"].shape[2]
    nheads = p["m_dt_bias"].shape[1]
    nstate = passes[0]["ssm0"].shape[-1]
    ngroups = (conv_dim - d_inner) // (2 * nstate)
    d_ff = p["f_w_down"].shape[1]
    nh_fox, dh = k_past.shape[2], k_past.shape[3]
    past = k_past.shape[1]

    def tables(s):
        return tuple(mod[s, :, i * d:(i + 1) * d] for i in range(3))

    def stack_rows(parts):
        return jnp.concatenate([a.reshape(-1, a.shape[-1]) for a in parts], axis=0)

    def stacked(stage, row_tile):
        rest = [stage(ps, out, None) for ps, out in zip(passes[1:], outs[1:])]
        more = stack_rows(rest)
        if passes[0]["nb"] == 1 and more.shape[0] % row_tile == 0:
            return stage(passes[0], outs[0], more)[0]
        return stack_rows([stage(passes[0], outs[0], None)] + rest)

    outs = [dict(ssm=[], conv=[], ffn=[]) for _ in passes]
    k_all = v_all = lf_pad = None
    cums = [None] * len(passes)
    for l in range(depth):
        shift, scale, gate = tables(2 * l)
        h = _norm_mod(x, scale, shift, bounds)
        if l < n_a:
            w_in = p["m_w_in"]
            zx = _matmul(h, w_in, layer=l, n=d_inner + conv_dim, bn=512)
            dt_raw = _matmul(h, w_in, layer=l, n_off=d_inner + conv_dim, n=nheads, bn=nheads)
            def mixer(ps, out, append):
                zxv, dtv = ps["view"](zx), ps["view"](dt_raw)
                xbc, tail = _causal_conv(zxv, ps["conv0"][l], p["m_conv_w"][l], p["m_conv_b"][l],
                                         boff=ps["boff"], t=ps["t"], col_off=d_inner, cdim=conv_dim,
                                         tc=2048, tt=256, out_dtype=F32)
                out["conv"].append(tail[:, SUBLANES - (p["m_conv_w"].shape[1] - 1):])
                y, sst = _ssd(xbc, zxv, dtv, p["m_dt_bias"][l], p["m_a_log"][l], p["m_d"][l],
                              p["m_norm_w"][l], ps["ssm0"][l], boff=ps["boff"], d_inner=d_inner,
                              ngroups=ngroups, nstate=nstate, chunk=min(SSD_CHUNK, ps["t"]),
                              append=append)
                out["ssm"].append(sst)
                return y

            y_all = stacked(mixer, _ssd_rows_per_step(passes[0]["t"], min(SSD_CHUNK, passes[0]["t"])))
            x = _matmul(y_all, p["m_w_out"], layer=l, bm=MM_ROWS_WIDE_K, bn=256, bk=d_inner,
                        epilogue="resid", res=x, gate_tbl=gate, bounds=bounds)
        else:
            j = l - n_a
            q = _matmul(h, p["w_q"], layer=j, out_dtype=MXU_DTYPE, bn=512, out_scale=dh ** -0.5 * LOG2E)
            os_ = []
            for ps, cum in zip(passes, cums):
                if ps["prefill"]:
                    qat, ka, vt = _fox_prep(ps["view"](q), ps["view"](k_all), ps["view"](v_all), cum[0],
                                            t=ps["t"], dh=dh)
                    o = _fox_prefill(qat, ka, vt, dh=dh)
                else:
                    o = _fox_decode(ps["view"](q), ps["k_past"], ps["v_past"], ps["view"](k_all),
                                    ps["view"](v_all), cum[0], cum[1], boff=ps["boff"], t=ps["t"], dh=dh)
                os_.append(o)
            x = _matmul(stack_rows(os_), p["w_o"], layer=j, bm=MM_ROWS_WIDE_K, bn=512,
                        epilogue="resid", res=x, gate_tbl=gate, bounds=bounds)
        shift, scale, gate = tables(2 * l + 1)
        h = _norm_mod(x, scale, shift, bounds)
        up = _matmul(h, p["f_w_up"], layer=l, bn=512)
        ffn_tt = 64

        def ffn_gate(ps, out, append):
            u, tail = _causal_conv(ps["view"](up), ps["ffn0"][l], p["f_conv_w"][l], p["f_conv_b"][l],
                                   boff=ps["boff"], t=ps["t"], col_off=d_ff, cdim=d_ff, tc=d_ff, tt=ffn_tt,
                                   out_dtype=MXU_DTYPE, gate_col_off=0, append=append)
            out["ffn"].append(tail[:, SUBLANES - (p["f_conv_w"].shape[1] - 1):])
            return u

        u_all = stacked(ffn_gate, _tile(passes[0]["t"], ffn_tt, SUBLANES))
        x = _matmul(u_all, p["f_w_down_lp"], layer=l, bm=MM_ROWS_WIDE_K, bn=256, bk=d_ff,
                    epilogue="resid", res=x, gate_tbl=gate, bounds=bounds)
        if l == n_a - 1:
            hkv = _norm_w(x[None], p["kv_norm_w"], boff=0, nb=1, t=r, out_dtype=MXU_DTYPE)[0]
            k_all = _matmul(hkv, p["w_kv"], n=d, bn=512)
            v_all = _matmul(hkv, p["w_kv"], n_off=d, n=d, bn=512)
            w_fg = jnp.pad(p["w_fgate"], ((0, 0), (0, LANES - nh_fox)))
            b_fg = jnp.pad(p["b_fgate"].astype(F32), (0, LANES - nh_fox)).reshape(1, LANES)
            lf_pad = _matmul(hkv, w_fg, bn=LANES, epilogue="logsig", bias=b_fg)
            tb = 256
            for i, ps in enumerate(passes):
                if ps["prefill"]:
                    assert ps["nb"] == 1 and ps["boff"] == 0 and ps["t"] % tb == 0
                    cums[i] = _cumsum_time(lf_pad[None], t=ps["t"], tb=tb)
                else:
                    lo = ps["boff"] * ps["t"]
                    lf_new = lf_pad[lo:lo + ps["nb"] * ps["t"]].reshape(ps["nb"], ps["t"], LANES)
                    lf_p = jnp.pad(ps["lf_past"].astype(F32), ((0, 0), (0, 0), (0, LANES - nh_fox)))
                    lf_all = jnp.concatenate([lf_p, lf_new], axis=1)
                    tot = lf_all.shape[1]
                    lf_all = jnp.pad(lf_all, ((0, 0), (0, (-tot) % tb), (0, 0)))
                    cums[i] = _cumsum_time(lf_all, t=lf_all.shape[1], tb=tb)
    results = []
    for ps, out in zip(passes, outs):
        lo, n = ps["boff"] * ps["t"], ps["nb"] * ps["t"]
        y_out = _norm_w(ps["view"](x), p["final_norm_w"], boff=ps["boff"], nb=ps["nb"], t=ps["t"],
                        out_dtype=x.dtype)
        seq = lambda a, w: a[lo:lo + n, :w].reshape((ps["nb"], ps["t"]) + ((nh_fox, dh) if w == d else (w,)))
        results.append((y_out, jnp.stack(out["ssm"]).astype(x.dtype), jnp.stack(out["conv"]),
                        jnp.stack(out["ffn"]), seq(k_all, d), seq(v_all, d),
                        seq(lf_pad, nh_fox).astype(x.dtype)))
    return results


def kernel(x_prompt, x_sample, c_prompt, c_sample, cache_k, cache_v, cache_logf, state_ssm, state_conv, state_ffn_conv, w_ada, b_ada, m_w_in, m_conv_w, m_conv_b, m_dt_bias, m_a_log, m_d, m_norm_w, m_w_out, kv_norm_w, w_kv, w_fgate, b_fgate, w_q, w_o, f_w_up, f_conv_w, f_conv_b, f_w_down, final_norm_w):
    p = dict(w_ada=w_ada, b_ada=b_ada, m_w_in=m_w_in, m_conv_w=m_conv_w, m_conv_b=m_conv_b,
             m_dt_bias=m_dt_bias, m_a_log=m_a_log, m_d=m_d, m_norm_w=m_norm_w, m_w_out=m_w_out,
             kv_norm_w=kv_norm_w, w_kv=w_kv, w_fgate=w_fgate, b_fgate=b_fgate, w_q=w_q, w_o=w_o,
             f_w_up=f_w_up, f_conv_w=f_conv_w, f_conv_b=f_conv_b, f_w_down=f_w_down,
             final_norm_w=final_norm_w)
    p["f_w_down_lp"] = f_w_down.astype(MXU_DTYPE)
    bp, tp, d = x_prompt.shape
    bs, ts, _ = x_sample.shape
    assert bp == 1 and tp % ts == 0 and ts % PACKED_ROWS == 0
    n_a, depth = m_w_in.shape[0], w_ada.shape[0]
    nheads, hd, nstate = state_ssm.shape[2], state_ssm.shape[3], state_ssm.shape[4]
    past = cache_k.shape[1]
    dtp = x_prompt.dtype
    r = tp + bs * ts

    nseq = bp + bs
    assert nseq <= SEQ_TABLE_ROWS
    c_rows = jnp.pad(jnp.concatenate([c_prompt, c_sample], axis=0), ((0, SEQ_TABLE_ROWS - nseq), (0, 0)))
    mod = _ada_all(c_rows, w_ada, b_ada)

    x = jnp.concatenate([x_prompt.reshape(tp, d), x_sample.reshape(bs * ts, d)], axis=0)
    bounds = ((0, tp, 0),) + tuple((tp + b * ts, tp + (b + 1) * ts, 1 + b) for b in range(bs))
    passes = [
        dict(prefill=True, boff=0, nb=1, t=tp, view=lambda a: a.reshape(1, r, a.shape[-1]),
             ssm0=jnp.zeros((n_a, bp, nheads, hd, nstate), F32),
             conv0=jnp.zeros((n_a, bp, m_conv_w.shape[1] - 1, m_conv_w.shape[2]), dtp),
             ffn0=jnp.zeros((depth, bp, f_conv_w.shape[1] - 1, f_conv_w.shape[2]), dtp)),
        dict(prefill=False, boff=tp // ts, nb=bs, t=ts, view=lambda a: a.reshape(r // ts, ts, a.shape[-1]),
             ssm0=state_ssm, conv0=state_conv, ffn0=state_ffn_conv, lf_past=cache_logf,
             k_past=cache_k, v_past=cache_v),
    ]
    out_p, out_s = _forward(x, mod, passes, bounds, cache_k, cache_v, cache_logf, p)
    return (out_p[0], out_s[0]) + out_p[1:] + out_s[1:]
```

```python
import functools
import math

import jax
import jax.numpy as jnp
from jax import lax
from jax.experimental import pallas as pl
from jax.experimental.pallas import tpu as pltpu

F32 = jnp.float32
MXU_DTYPE = jnp.bfloat16
EPS = 1e-6
LANES = 128
SUBLANES = 8
PACKED_ROWS = 16
VMEM_LIMIT_BYTES = 56 * 1024 * 1024
SSD_CHUNK = 64
SSD_CHUNKS_PER_STEP = 2
NEG_INF = float("-inf")
LOG2E = math.log2(math.e)
MM_ROWS = 2080
MM_ROWS_WIDE_K = 1040
NORM_ROWS = 320
SEQ_TABLE_ROWS = 16


def _tile(dim, pref, align):
    t = (min(pref, dim) // align) * align
    while t >= align:
        if dim % t == 0:
            return t
        t -= align
    return dim


def _params(*sem):
    return pltpu.CompilerParams(dimension_semantics=sem, vmem_limit_bytes=VMEM_LIMIT_BYTES)


def _silu(x):
    return x * jax.nn.sigmoid(x)


def _softplus(x):
    return jnp.maximum(x, 0.0) + jnp.log1p(jnp.exp(-jnp.abs(x)))


def _split3(a):
    a1 = a.astype(MXU_DTYPE)
    r1 = a - a1.astype(F32)
    a2 = r1.astype(MXU_DTYPE)
    r2 = r1 - a2.astype(F32)
    return a1, a2, r2.astype(MXU_DTYPE)


def _dot01(m01, a, dims=(((1,), (0,)), ((), ()))):
    m = m01.astype(MXU_DTYPE)
    out = None
    for t in _split3(a):
        part = lax.dot_general(m, t, dims, preferred_element_type=F32)
        out = part if out is None else out + part
    return out


def _dot01r(a, m01):
    out = None
    for t in _split3(a):
        part = jnp.dot(t, m01.astype(MXU_DTYPE), preferred_element_type=F32)
        out = part if out is None else out + part
    return out


def _tile_segments(rows_per_tile, ntiles, bounds):
    out = []
    for i in range(ntiles):
        lo, hi = i * rows_per_tile, (i + 1) * rows_per_tile
        out.append(tuple((max(s, lo) - lo, min(e, hi) - lo, q) for s, e, q in bounds
                         if max(s, lo) < min(e, hi)))
    return tuple(out)


def _for_tile_segments(i, tile_segs, emit):
    groups = {}
    for t, segs in enumerate(tile_segs):
        groups.setdefault(segs, []).append(t)
    if len(groups) == 1:
        emit(tile_segs[0])
        return
    for segs, tiles in groups.items():
        runs = []
        for t in tiles:
            if runs and runs[-1][1] == t - 1:
                runs[-1][1] = t
            else:
                runs.append([t, t])
        cond = None
        for a, b in runs:
            c = (i == a) if a == b else ((i >= a) & (i <= b))
            cond = c if cond is None else (cond | c)
        pl.when(cond)(functools.partial(emit, segs))


def _ada_kernel(c_ref, w_ref, b_ref, o_ref):
    a = _silu(c_ref[...]).astype(MXU_DTYPE)
    w = w_ref[...].astype(MXU_DTYPE)
    o_ref[...] = jnp.dot(a, w, preferred_element_type=F32) + b_ref[...]


def _ada_all(c_rows, w_ada, b_ada):
    r, d = c_rows.shape
    s = w_ada.shape[0] * w_ada.shape[1]
    n = w_ada.shape[-1]
    w = w_ada.reshape(s, d, n)
    b = b_ada.reshape(s, 1, n)
    tn = _tile(n, 512, LANES)
    return pl.pallas_call(
        _ada_kernel,
        out_shape=jax.ShapeDtypeStruct((s, r, n), F32),
        grid=(s, n // tn),
        in_specs=[
            pl.BlockSpec((r, d), lambda i, j: (0, 0)),
            pl.BlockSpec((None, d, tn), lambda i, j: (i, 0, j)),
            pl.BlockSpec((None, 1, tn), lambda i, j: (i, 0, j)),
        ],
        out_specs=pl.BlockSpec((None, r, tn), lambda i, j: (i, 0, j)),
        compiler_params=_params("parallel", "parallel"),
        name="ada_mod",
    )(c_rows, w, b)


def _rms(x):
    return x * lax.rsqrt(jnp.mean(x * x, axis=-1, keepdims=True) + EPS)


def _norm_mod_kernel(x_ref, scale_ref, shift_ref, o_ref, *, tile_segs):
    y = _rms(x_ref[...])

    def emit(segs):
        for lo, hi, s in segs:
            v = y[lo:hi, :] * (1.0 + scale_ref[s:s + 1, :]) + shift_ref[s:s + 1, :]
            o_ref[lo:hi, :] = v.astype(o_ref.dtype)

    _for_tile_segments(pl.program_id(0), tile_segs, emit)


def _norm_mod(x, scale_tbl, shift_tbl, bounds):
    r, d = x.shape
    tt = _tile(r, NORM_ROWS, PACKED_ROWS)
    nt = r // tt
    s = scale_tbl.shape[0]
    return pl.pallas_call(
        functools.partial(_norm_mod_kernel, tile_segs=_tile_segments(tt, nt, bounds)),
        out_shape=jax.ShapeDtypeStruct((r, d), MXU_DTYPE),
        grid=(nt,),
        in_specs=[pl.BlockSpec((tt, d), lambda i: (i, 0)),
                  pl.BlockSpec((s, d), lambda i: (0, 0)),
                  pl.BlockSpec((s, d), lambda i: (0, 0))],
        out_specs=pl.BlockSpec((tt, d), lambda i: (i, 0)),
        compiler_params=_params("parallel"),
        name="rmsnorm_mod",
    )(x, scale_tbl, shift_tbl)


def _norm_w_kernel(x_ref, w_ref, o_ref):
    o_ref[...] = (_rms(x_ref[...]) * w_ref[...]).astype(o_ref.dtype)


def _norm_w(xv, w, *, boff, nb, t, out_dtype):
    d = xv.shape[-1]
    tt = _tile(t, NORM_ROWS, PACKED_ROWS)
    return pl.pallas_call(
        _norm_w_kernel,
        out_shape=jax.ShapeDtypeStruct((nb, t, d), out_dtype),
        grid=(nb, t // tt),
        in_specs=[pl.BlockSpec((None, tt, d), lambda i, j: (i + boff, j, 0)),
                  pl.BlockSpec((1, d), lambda i, j: (0, 0))],
        out_specs=pl.BlockSpec((None, tt, d), lambda i, j: (i, j, 0)),
        compiler_params=_params("parallel", "parallel"),
        name="rmsnorm_w",
    )(xv, w.astype(F32).reshape(1, d))


def _mm_kernel(*refs, nk, epilogue, out_scale, tile_segs):
    x_ref, w_ref = refs[0], refs[1]
    pos = 2
    if epilogue == "resid":
        res_ref, gate_ref = refs[2], refs[3]
        pos = 4
    elif epilogue == "logsig":
        bias_ref = refs[2]
        pos = 3
    o_ref = refs[pos]
    part = jnp.dot(x_ref[...], w_ref[...].astype(MXU_DTYPE), preferred_element_type=F32)

    def finish(acc):
        if out_scale is not None:
            acc = acc * out_scale
        if epilogue == "resid":
            def emit(segs):
                for lo, hi, s in segs:
                    v = res_ref[lo:hi, :] + gate_ref[s:s + 1, :] * acc[lo:hi, :]
                    o_ref[lo:hi, :] = v.astype(o_ref.dtype)

            _for_tile_segments(pl.program_id(0), tile_segs, emit)
            return
        if epilogue == "logsig":
            acc = -_softplus(-(acc + bias_ref[...]))
        o_ref[...] = acc.astype(o_ref.dtype)

    if nk == 1:
        finish(part)
    else:
        acc_ref = refs[pos + 1]
        k = pl.program_id(2)

        @pl.when(k == 0)
        def _():
            acc_ref[...] = part

        @pl.when(k > 0)
        def _():
            acc_ref[...] += part

        @pl.when(k == nk - 1)
        def _():
            finish(acc_ref[...])


def _matmul(x, w, *, layer=0, n_off=0, n=None, out_dtype=F32, bm=None, bn=512, bk=4096,
            epilogue=None, res=None, gate_tbl=None, bounds=None, bias=None, out_scale=None):
    m, kdim = x.shape
    if w.ndim == 2:
        w = w[None]
    n = w.shape[2] if n is None else n
    bm = _tile(m, MM_ROWS if bm is None else bm, PACKED_ROWS)
    bn = _tile(math.gcd(n, n_off), bn, LANES)
    bk = _tile(kdim, bk, LANES)
    joff = n_off // bn
    nk = kdim // bk
    ins = [x, w]
    x_mode = dict(pipeline_mode=pl.Buffered(1)) if nk == 1 and n // bn > 1 else {}
    specs = [pl.BlockSpec((bm, bk), lambda i, j, k: (i, k), **x_mode),
             pl.BlockSpec((None, bk, bn), lambda i, j, k: (layer, k, j + joff))]
    tile_segs = None
    if epilogue == "resid":
        ins += [res, gate_tbl]
        specs.append(pl.BlockSpec((bm, bn), lambda i, j, k: (i, j)))
        specs.append(pl.BlockSpec((gate_tbl.shape[0], bn), lambda i, j, k: (0, j)))
        tile_segs = _tile_segments(bm, m // bm, bounds)
    elif epilogue == "logsig":
        ins.append(bias)
        specs.append(pl.BlockSpec((1, bn), lambda i, j, k: (0, j)))
    scratch = [pltpu.VMEM((bm, bn), F32)] if nk > 1 else []
    return pl.pallas_call(
        functools.partial(_mm_kernel, nk=nk, epilogue=epilogue, out_scale=out_scale,
                          tile_segs=tile_segs),
        out_shape=jax.ShapeDtypeStruct((m, n), out_dtype),
        grid=(m // bm, n // bn, nk),
        in_specs=specs,
        out_specs=pl.BlockSpec((bm, bn), lambda i, j, k: (i, j)),
        scratch_shapes=scratch,
        compiler_params=_params("parallel", "parallel", "arbitrary"),
        name="matmul_" + (epilogue or "plain"),
    )(*ins)


def _conv_kernel(*refs, width, tt, gated, nt, appending):
    refs = list(refs)
    a_ref = refs.pop(0) if gated else None
    x_ref, buf_ref, w_ref, b_ref = refs[:4]
    more_ref = refs[4] if appending else None
    y_ref, tail_ref, xx_ref = refs[-3:]
    ti = pl.program_id(2)

    @pl.when(ti == 0)
    def _():
        xx_ref[0:SUBLANES, :] = buf_ref[...]

    @pl.when(ti < nt)
    def _():
        xx_ref[SUBLANES:SUBLANES + tt, :] = x_ref[...]
        acc = b_ref[...]
        for k in range(width):
            lo = SUBLANES - (width - 1) + k
            acc = acc + w_ref[k:k + 1, :] * xx_ref[lo:lo + tt, :]
        y = _silu(acc)
        if gated:
            y = y * a_ref[...]
        y_ref[...] = y.astype(y_ref.dtype)
        tail = xx_ref[tt:tt + SUBLANES, :]
        xx_ref[0:SUBLANES, :] = tail
        tail_ref[...] = tail

    if appending:
        @pl.when(ti >= nt)
        def _():
            y_ref[...] = more_ref[...]


def _causal_conv(srcv, buf, w, b, *, boff, t, col_off, cdim, tc, tt, out_dtype, gate_col_off=None,
                 append=None):
    nb = buf.shape[0]
    width = w.shape[0]
    assert t >= SUBLANES and width - 1 <= SUBLANES
    tt = _tile(t, tt, SUBLANES)
    tc = _tile(math.gcd(cdim, col_off, gate_col_off or 0), tc, LANES)
    coff = col_off // tc
    nt = t // tt
    extra = 0 if append is None else append.shape[0]
    assert extra % tt == 0 and (extra == 0 or nb == 1)
    last = lambda k: jnp.minimum(k, nt - 1)
    buf8 = jnp.pad(buf.astype(F32), ((0, 0), (SUBLANES - (width - 1), 0), (0, 0)))
    w8 = jnp.pad(w, ((0, SUBLANES - width), (0, 0)))
    b2 = b.reshape(1, cdim)
    gated = gate_col_off is not None
    ins, specs = [], []
    if gated:
        goff = gate_col_off // tc
        ins.append(srcv)
        specs.append(pl.BlockSpec((None, tt, tc), lambda i, j, k: (i + boff, last(k), j + goff)))
    ins += [srcv, buf8, w8, b2]
    specs += [
        pl.BlockSpec((None, tt, tc), lambda i, j, k: (i + boff, last(k), j + coff)),
        pl.BlockSpec((None, SUBLANES, tc), lambda i, j, k: (i, 0, j)),
        pl.BlockSpec((SUBLANES, tc), lambda i, j, k: (0, j)),
        pl.BlockSpec((1, tc), lambda i, j, k: (0, j)),
    ]
    if extra:
        ins.append(append)
        specs.append(pl.BlockSpec((tt, tc), lambda i, j, k: (jnp.maximum(k - nt, 0), j)))
    return pl.pallas_call(
        functools.partial(_conv_kernel, width=width, tt=tt, gated=gated, nt=nt, appending=extra > 0),
        out_shape=(jax.ShapeDtypeStruct((nb, t + extra, cdim), out_dtype),
                   jax.ShapeDtypeStruct((nb, SUBLANES, cdim), F32)),
        grid=(nb, cdim // tc, nt + extra // tt),
        in_specs=specs,
        out_specs=(pl.BlockSpec((None, tt, tc), lambda i, j, k: (i, k, j)),
                   pl.BlockSpec((None, SUBLANES, tc), lambda i, j, k: (i, 0, j))),
        scratch_shapes=[pltpu.VMEM((tt + SUBLANES, tc), F32)],
        compiler_params=_params("parallel", "parallel", "arbitrary"),
        name="causal_conv_gated" if gated else "causal_conv",
    )(*ins)


def _ssd_kernel(*refs, nc, appending, **kw):
    if not appending:
        _ssd_chunk(*refs, nc=nc, **kw)
        return
    more_ref, y_ref = refs[11], refs[12]
    c = pl.program_id(2)

    @pl.when(c < nc)
    def _():
        _ssd_chunk(*(refs[:11] + refs[12:]), nc=nc, **kw)

    @pl.when(c >= nc)
    def _():
        y_ref[...] = more_ref[...]


def _ssd_chunk(x_ref, b_ref, c_ref, z_ref, dt_ref, dtb_ref, alog_ref, dsk_ref, nw_ref, e_ref, h0_ref,
               y_ref, hout_ref, ht_ref, yg_ref, *, hpg, hd, ln, nheads, nc, cps):
    g = pl.program_id(1)
    c = pl.program_id(2)
    gw = hpg * hd
    per_blk = LANES // hd
    nblk = gw // LANES
    nblk_p = max(nblk, SUBLANES)
    nt = (((1,), (1,)), ((), ()))

    @pl.when(c == 0)
    def _():
        ht_ref[...] = h0_ref[...].T

    a_neg = -jnp.exp(alog_ref[...])
    shift = lax.rem(nheads - g * hpg, nheads)
    dsk_g = pltpu.roll(jnp.broadcast_to(dsk_ref[...], (SUBLANES, nheads)), shift, axis=1)
    expand = e_ref[...]

    tril = (lax.broadcasted_iota(jnp.int32, (ln, ln), 0) >= lax.broadcasted_iota(jnp.int32, (ln, ln), 1))
    rowi = lax.broadcasted_iota(jnp.int32, (ln, LANES), 0)
    lanei = lax.broadcasted_iota(jnp.int32, (ln, LANES), 1)
    in_slot = lanei & (hd - 1)
    slot_causal = (in_slot <= rowi) & (in_slot < ln)
    slot_of_lane = [(lanei >= w * hd) & (lanei < (w + 1) * hd) for w in range(per_blk)]
    sel_blk = [lax.broadcasted_iota(jnp.int32, (nblk_p, nheads), 1)
               == per_blk * lax.broadcasted_iota(jnp.int32, (nblk_p, nheads), 0) + w
               for w in range(per_blk)]

    def in_slots(parts):
        rows = []
        for part in parts:
            rows.append(part)
            if hd > ln:
                rows.append(jnp.zeros((hd - ln, part.shape[1]), part.dtype))
        return jnp.concatenate(rows, axis=0)

    for k in range(cps):
        r0 = k * ln
        dt = _softplus(dt_ref[r0:r0 + ln, :] + dtb_ref[...])
        dt_g = pltpu.roll(dt, shift, axis=1)
        da_g = pltpu.roll(dt * a_neg, shift, axis=1)
        a_cum = _dot01(tril, da_g)
        ex = _dot01r(jnp.concatenate([a_cum, dt_g, dsk_g], axis=0), expand)
        acol_all, dcol_all, dsk_all = ex[0:ln], ex[ln:2 * ln], ex[2 * ln:2 * ln + 1]
        a_rows = None
        zero_rows = jnp.zeros((ln, nheads), F32)
        for w in range(per_blk):
            placed = in_slots([a_cum if v == w else zero_rows for v in range(per_blk)])
            part = _dot01(sel_blk[w], placed, nt)
            a_rows = part if a_rows is None else a_rows + part

        bmat = b_ref[r0:r0 + ln, :].astype(MXU_DTYPE)
        cmat = c_ref[r0:r0 + ln, :].astype(MXU_DTYPE)
        cb = lax.dot_general(cmat, in_slots([bmat] * per_blk), nt,
                             preferred_element_type=F32)

        ysq = jnp.zeros((ln, LANES), F32)
        for j in range(nblk):
            lo, hi = j * LANES, (j + 1) * LANES
            x_blk = x_ref[r0:r0 + ln, lo:hi]
            acol, dcol = acol_all[:, lo:hi], dcol_all[:, lo:hi]
            alast = acol[ln - 1:ln, :]
            xdt = x_blk * dcol
            xw_m = (xdt * jnp.exp(alast - acol)).astype(MXU_DTYPE)
            dec = jnp.exp(jnp.where(slot_causal, acol - a_rows[j:j + 1, :], NEG_INF))
            xbd = in_slots([jnp.where(slot_of_lane[w], xdt, 0.0) for w in range(per_blk)])
            xbd = xbd.astype(MXU_DTYPE)
            ydiag = jnp.dot((cb * dec).astype(MXU_DTYPE), xbd, preferred_element_type=F32)
            h_blk = ht_ref[:, lo:hi]
            yoff = jnp.dot(cmat, h_blk.astype(MXU_DTYPE), preferred_element_type=F32) * jnp.exp(acol)
            st = lax.dot_general(bmat, xw_m, (((0,), (0,)), ((), ())), preferred_element_type=F32)
            ht_ref[:, lo:hi] = h_blk * jnp.exp(alast) + st
            y = ydiag + yoff + dsk_all[:, lo:hi] * x_blk
            y = y * _silu(z_ref[r0:r0 + ln, lo:hi])
            yg_ref[r0:r0 + ln, lo:hi] = y
            ysq = ysq + y * y

        inv = lax.rsqrt(jnp.sum(ysq, axis=-1, keepdims=True) * (1.0 / gw) + EPS)
        y_ref[r0:r0 + ln, :] = (yg_ref[r0:r0 + ln, :] * inv * nw_ref[...]).astype(y_ref.dtype)

    @pl.when(c == nc - 1)
    def _():
        hout_ref[...] = ht_ref[...].T


def _ssd_rows_per_step(t, chunk):
    cps = SSD_CHUNKS_PER_STEP if t % (SSD_CHUNKS_PER_STEP * chunk) == 0 else 1
    return cps * chunk


def _ssd(xbc, zxv, dtv, dt_bias, a_log, d_skip, norm_w, h0, *, boff, d_inner, ngroups, nstate, chunk,
         append=None):
    nb, t, _ = xbc.shape
    nheads = dtv.shape[-1]
    hd = d_inner // nheads
    hpg = nheads // ngroups
    gw = hpg * hd
    assert nheads <= LANES
    assert LANES % hd == 0 and gw % LANES == 0 and nstate % LANES == 0 and t % chunk == 0
    assert hd & (hd - 1) == 0 and chunk <= hd
    rows = _ssd_rows_per_step(t, chunk)
    cps = rows // chunk
    nc = t // rows
    h0g = h0.astype(F32).reshape(nb, ngroups, gw, nstate)
    vec = lambda a: a.astype(F32).reshape(1, nheads)
    xoff = d_inner // nstate
    extra = 0 if append is None else append.shape[0]
    assert extra % rows == 0 and (extra == 0 or nb == 1)
    last = lambda c: jnp.minimum(c, nc - 1)
    kern = functools.partial(_ssd_kernel, hpg=hpg, hd=hd, ln=chunk, nheads=nheads, nc=nc, cps=cps,
                             appending=extra > 0)
    expand = (jnp.arange(nheads)[:, None] == jnp.arange(gw)[None, :] // hd).astype(MXU_DTYPE)
    ins = [xbc, xbc, xbc, zxv, dtv, vec(dt_bias), vec(a_log), vec(d_skip),
           norm_w.astype(F32).reshape(1, d_inner), expand, h0g]
    specs = [
        pl.BlockSpec((None, rows, gw), lambda b, g, c: (b, last(c), g)),
        pl.BlockSpec((None, rows, nstate), lambda b, g, c: (b, last(c), xoff + g)),
        pl.BlockSpec((None, rows, nstate), lambda b, g, c: (b, last(c), xoff + ngroups + g)),
        pl.BlockSpec((None, rows, gw), lambda b, g, c: (b + boff, last(c), g)),
        pl.BlockSpec((None, rows, nheads), lambda b, g, c: (b + boff, last(c), 0)),
        pl.BlockSpec((1, nheads), lambda b, g, c: (0, 0)),
        pl.BlockSpec((1, nheads), lambda b, g, c: (0, 0)),
        pl.BlockSpec((1, nheads), lambda b, g, c: (0, 0)),
        pl.BlockSpec((1, gw), lambda b, g, c: (0, g)),
        pl.BlockSpec((nheads, gw), lambda b, g, c: (0, 0)),
        pl.BlockSpec((None, None, gw, nstate), lambda b, g, c: (b, g, 0, 0)),
    ]
    if extra:
        ins.append(append)
        specs.append(pl.BlockSpec((rows, gw), lambda b, g, c: (jnp.maximum(c - nc, 0), g)))
    y, hout = pl.pallas_call(
        kern,
        out_shape=(jax.ShapeDtypeStruct((nb, t + extra, d_inner), MXU_DTYPE),
                   jax.ShapeDtypeStruct((nb, ngroups, gw, nstate), F32)),
        grid=(nb, ngroups, nc + extra // rows),
        in_specs=specs,
        out_specs=(pl.BlockSpec((None, rows, gw), lambda b, g, c: (b, c, g)),
                   pl.BlockSpec((None, None, gw, nstate), lambda b, g, c: (b, g, 0, 0))),
        scratch_shapes=[pltpu.VMEM((nstate, gw), F32), pltpu.VMEM((rows, gw), F32)],
        compiler_params=_params("parallel", "parallel", "arbitrary"),
        name="ssd_scan",
    )(*ins)
    return y, hout.reshape(nb, nheads, hd, nstate)


def _cumsum_kernel(lf_ref, c_ref, ct_ref, carry_ref, *, tb):
    @pl.when(pl.program_id(1) == 0)
    def _():
        carry_ref[...] = jnp.zeros_like(carry_ref)

    row = lax.broadcasted_iota(jnp.int32, (tb, tb), 0)
    col = lax.broadcasted_iota(jnp.int32, (tb, tb), 1)
    cs = _dot01(row >= col, lf_ref[...]) + carry_ref[0:1, :]
    c_ref[...] = cs
    ct_ref[...] = cs.T
    carry_ref[...] = jnp.broadcast_to(cs[tb - 1:tb, :], carry_ref.shape)


def _cumsum_time(lfv, *, t, tb):
    bsz, _, w = lfv.shape
    return pl.pallas_call(
        functools.partial(_cumsum_kernel, tb=tb),
        out_shape=(jax.ShapeDtypeStruct((bsz, t, w), F32), jax.ShapeDtypeStruct((bsz, w, t), F32)),
        grid=(bsz, t // tb),
        in_specs=[pl.BlockSpec((None, tb, w), lambda b, i: (b, i, 0))],
        out_specs=(pl.BlockSpec((None, tb, w), lambda b, i: (b, i, 0)),
                   pl.BlockSpec((None, w, tb), lambda b, i: (b, 0, i))),
        scratch_shapes=[pltpu.VMEM((SUBLANES, w), F32)],
        compiler_params=_params("parallel", "arbitrary"),
        name="logf_cumsum",
    )(lfv)


def _head_column(c_blk, h, width):
    rolled = pltpu.roll(c_blk, lax.rem(LANES - h, LANES), axis=1)
    return jnp.broadcast_to(rolled[:, 0:1], (c_blk.shape[0], width))


def _aug_cols(c_blk, h, for_query):
    col = _head_column(c_blk, h, LANES) * LOG2E
    c1, c2, c3 = (t.astype(F32) for t in _split3(col))
    one = jnp.ones_like(col)
    terms = (c1, c2, c3, one, one, one) if for_query else (one, one, one, -c1, -c2, -c3)
    lane = lax.broadcasted_iota(jnp.int32, col.shape, 1)
    out = jnp.zeros_like(col)
    for idx, term in enumerate(terms):
        out = jnp.where(lane == idx, term, out)
    return out


def _fox_prep_kernel(q_ref, k_ref, v_ref, c_ref, qat_ref, ka_ref, vt_ref):
    h = pl.program_id(2)
    dh = q_ref.shape[-1]
    c_blk = c_ref[...]
    ka_ref[:, 0:dh] = k_ref[...].astype(MXU_DTYPE)
    ka_ref[:, dh:dh + LANES] = _aug_cols(c_blk, h, False).astype(MXU_DTYPE)
    qat_ref[0:dh, :] = q_ref[...].astype(F32).T.astype(MXU_DTYPE)
    qat_ref[dh:dh + LANES, :] = _aug_cols(c_blk, h, True).T.astype(MXU_DTYPE)
    vt_ref[...] = v_ref[...].T.astype(MXU_DTYPE)


def _fox_prep(qv, kv, vv, c, *, t, dh):
    bsz, _, d = qv.shape
    nh = d // dh
    tt = _tile(t, 1024, LANES)
    ka_w = dh + LANES
    return pl.pallas_call(
        _fox_prep_kernel,
        out_shape=(jax.ShapeDtypeStruct((bsz, nh, ka_w, t), MXU_DTYPE),
                   jax.ShapeDtypeStruct((bsz, nh, t, ka_w), MXU_DTYPE),
                   jax.ShapeDtypeStruct((bsz, nh, dh, t), MXU_DTYPE)),
        grid=(bsz, t // tt, nh),
        in_specs=[
            pl.BlockSpec((None, tt, dh), lambda b, i, h: (b, i, h)),
            pl.BlockSpec((None, tt, dh), lambda b, i, h: (b, i, h)),
            pl.BlockSpec((None, tt, dh), lambda b, i, h: (b, i, h)),
            pl.BlockSpec((None, tt, LANES), lambda b, i, h: (b, i, 0)),
        ],
        out_specs=(pl.BlockSpec((None, None, ka_w, tt), lambda b, i, h: (b, h, 0, i)),
                   pl.BlockSpec((None, None, tt, ka_w), lambda b, i, h: (b, h, i, 0)),
                   pl.BlockSpec((None, None, dh, tt), lambda b, i, h: (b, h, 0, i))),
        compiler_params=_params("parallel", "parallel", "parallel"),
        name="fox_prep",
    )(qv, kv, vv, c)


def _fox_prefill_kernel(qi_tbl, ki_tbl, qat_ref, ka_ref, vt_ref, o_ref, m_ref, l_ref, acc_ref, *, tq):
    step_id = pl.program_id(2)
    qi = qi_tbl[step_id]
    ki = ki_tbl[step_id]

    @pl.when(ki == 0)
    def _():
        m_ref[...] = jnp.full_like(m_ref, NEG_INF)
        l_ref[...] = jnp.zeros_like(l_ref)
        acc_ref[...] = jnp.zeros_like(acc_ref)

    def step(diagonal):
        s = jnp.dot(ka_ref[...], qat_ref[...], preferred_element_type=F32)
        if diagonal:
            krow = lax.broadcasted_iota(jnp.int32, (tq, tq), 0)
            qcol = lax.broadcasted_iota(jnp.int32, (tq, tq), 1)
            s = jnp.where(krow <= qcol, s, NEG_INF)
        m_prev = m_ref[...]
        m_new = jnp.maximum(m_prev, jnp.max(s, axis=0, keepdims=True))
        p = jnp.exp2(s - m_new)
        alpha = jnp.exp2(m_prev - m_new)
        l_ref[...] = alpha * l_ref[...] + jnp.sum(p, axis=0, keepdims=True)
        acc_ref[...] = alpha * acc_ref[...] + jnp.dot(
            vt_ref[...], p.astype(MXU_DTYPE), preferred_element_type=F32)
        m_ref[...] = m_new

    @pl.when(ki < qi)
    def _():
        step(False)

    @pl.when(ki == qi)
    def _():
        step(True)
        o_ref[...] = (acc_ref[...] / l_ref[...]).T.astype(o_ref.dtype)


def _fox_prefill(qat, ka, vt, *, dh):
    bsz, nh, ka_w, t = qat.shape
    tq = _tile(t, 1024, LANES)
    nq = t // tq
    pairs = [(qi, ki) for qi in range(nq) for ki in range(qi + 1)]
    qi_tbl = jnp.asarray([pr[0] for pr in pairs], jnp.int32)
    ki_tbl = jnp.asarray([pr[1] for pr in pairs], jnp.int32)
    grid_spec = pltpu.PrefetchScalarGridSpec(
        num_scalar_prefetch=2,
        grid=(bsz, nh, len(pairs)),
        in_specs=[
            pl.BlockSpec((None, None, ka_w, tq), lambda b, h, s, qt, kt: (b, h, 0, qt[s])),
            pl.BlockSpec((None, None, tq, ka_w), lambda b, h, s, qt, kt: (b, h, kt[s], 0)),
            pl.BlockSpec((None, None, dh, tq), lambda b, h, s, qt, kt: (b, h, 0, kt[s])),
        ],
        out_specs=pl.BlockSpec((None, tq, dh), lambda b, h, s, qt, kt: (b, qt[s], h)),
        scratch_shapes=[pltpu.VMEM((1, tq), F32), pltpu.VMEM((1, tq), F32), pltpu.VMEM((dh, tq), F32)],
    )
    return pl.pallas_call(
        functools.partial(_fox_prefill_kernel, tq=tq),
        out_shape=jax.ShapeDtypeStruct((bsz, t, nh * dh), MXU_DTYPE),
        grid_spec=grid_spec,
        compiler_params=_params("parallel", "parallel", "arbitrary"),
        name="fox_prefill",
    )(qi_tbl, ki_tbl, qat, ka, vt)


def _fox_decode_kernel(q_ref, kp_ref, vp_ref, kn_ref, vn_ref, cq_ref, ckt_ref, o_ref, *, past, tq, dh):
    hg = pl.program_id(1)
    nt = (((1,), (1,)), ((), ()))
    row = lax.broadcasted_iota(jnp.int32, (tq, tq), 0)
    col = lax.broadcasted_iota(jnp.int32, (tq, tq), 1)
    cq_blk = cq_ref[...]
    for hh in range(SUBLANES):
        h = hg * SUBLANES + hh
        lo, hi = hh * dh, (hh + 1) * dh
        q = q_ref[:, lo:hi]
        ck = ckt_ref[pl.ds(h, 1), :]
        cq_p = _head_column(cq_blk, h, past)
        s_p = lax.dot_general(q, kp_ref[:, hh, :].astype(MXU_DTYPE), nt, preferred_element_type=F32)
        s_p = s_p + (cq_p - ck[:, 0:past]) * LOG2E
        s_n = lax.dot_general(q, kn_ref[:, lo:hi].astype(MXU_DTYPE), nt, preferred_element_type=F32)
        s_n = s_n + (cq_p[:, 0:tq] - ck[:, past:past + tq]) * LOG2E
        s_n = jnp.where(col <= row, s_n, NEG_INF)
        m = jnp.maximum(jnp.max(s_p, axis=-1, keepdims=True), jnp.max(s_n, axis=-1, keepdims=True))
        p_p = jnp.exp2(s_p - m)
        p_n = jnp.exp2(s_n - m)
        l = jnp.sum(p_p, axis=-1, keepdims=True) + jnp.sum(p_n, axis=-1, keepdims=True)
        acc = jnp.dot(p_p.astype(MXU_DTYPE), vp_ref[:, hh, :].astype(MXU_DTYPE), preferred_element_type=F32)
        acc = acc + jnp.dot(p_n.astype(MXU_DTYPE), vn_ref[:, lo:hi].astype(MXU_DTYPE),
                            preferred_element_type=F32)
        o_ref[:, lo:hi] = (acc / l).astype(o_ref.dtype)


def _fox_decode(qv, k_past, v_past, knv, vnv, c, ct, *, boff, t, dh):
    nb, past, nh, _ = k_past.shape
    d = nh * dh
    tpad = ct.shape[-1]
    assert past % t == 0 and past % LANES == 0 and nh % SUBLANES == 0
    hgroups = nh // SUBLANES
    gwid = SUBLANES * dh
    tiled = lambda a: a.reshape(nb, past, hgroups, SUBLANES, dh)
    new_blk = lambda b, hg: (b + boff, 0, hg)
    past_blk = pl.BlockSpec((None, past, None, SUBLANES, dh), lambda b, hg: (b, 0, hg, 0, 0))
    return pl.pallas_call(
        functools.partial(_fox_decode_kernel, past=past, tq=t, dh=dh),
        out_shape=jax.ShapeDtypeStruct((nb, t, d), MXU_DTYPE),
        grid=(nb, hgroups),
        in_specs=[
            pl.BlockSpec((None, t, gwid), new_blk),
            past_blk,
            past_blk,
            pl.BlockSpec((None, t, gwid), new_blk),
            pl.BlockSpec((None, t, gwid), new_blk),
            pl.BlockSpec((None, t, LANES), lambda b, hg: (b, past // t, 0)),
            pl.BlockSpec((None, LANES, tpad), lambda b, hg: (b, 0, 0)),
        ],
        out_specs=pl.BlockSpec((None, t, gwid), lambda b, hg: (b, 0, hg)),
        compiler_params=_params("parallel", "parallel"),
        name="fox_decode",
    )(qv, tiled(k_past), tiled(v_past), knv, vnv, c, ct)


def _forward(x, mod, passes, bounds, k_past, v_past, lf_past, p):
    r, d = x.shape
    depth = p["w_ada"].shape[0]
    n_a = p["m_w_in"].shape[0]
    d_inner = p["m_w_out"].shape[1]
    conv_dim = p["m_conv_w"].shape[2]
    nheads = p["m_dt_bias"].shape[1]
    nstate = passes[0]["ssm0"].shape[-1]
    ngroups = (conv_dim - d_inner) // (2 * nstate)
    d_ff = p["f_w_down"].shape[1]
    nh_fox, dh = k_past.shape[2], k_past.shape[3]
    past = k_past.shape[1]

    def tables(s):
        return tuple(mod[s, :, i * d:(i + 1) * d] for i in range(3))

    def stack_rows(parts):
        return jnp.concatenate([a.reshape(-1, a.shape[-1]) for a in parts], axis=0)

    def stacked(stage, row_tile):
        rest = [stage(ps, out, None) for ps, out in zip(passes[1:], outs[1:])]
        more = stack_rows(rest)
        if passes[0]["nb"] == 1 and more.shape[0] % row_tile == 0:
            return stage(passes[0], outs[0], more)[0]
        return stack_rows([stage(passes[0], outs[0], None)] + rest)

    outs = [dict(ssm=[], conv=[], ffn=[]) for _ in passes]
    k_all = v_all = lf_pad = None
    cums = [None] * len(passes)
    for l in range(depth):
        shift, scale, gate = tables(2 * l)
        h = _norm_mod(x, scale, shift, bounds)
        if l < n_a:
            w_in = p["m_w_in"]
            zx = _matmul(h, w_in, layer=l, n=d_inner + conv_dim, bn=512)
            dt_raw = _matmul(h, w_in, layer=l, n_off=d_inner + conv_dim, n=nheads, bn=nheads)
            def mixer(ps, out, append):
                zxv, dtv = ps["view"](zx), ps["view"](dt_raw)
                xbc, tail = _causal_conv(zxv, ps["conv0"][l], p["m_conv_w"][l], p["m_conv_b"][l],
                                         boff=ps["boff"], t=ps["t"], col_off=d_inner, cdim=conv_dim,
                                         tc=2048, tt=512, out_dtype=F32)
                out["conv"].append(tail[:, SUBLANES - (p["m_conv_w"].shape[1] - 1):])
                y, sst = _ssd(xbc, zxv, dtv, p["m_dt_bias"][l], p["m_a_log"][l], p["m_d"][l],
                              p["m_norm_w"][l], ps["ssm0"][l], boff=ps["boff"], d_inner=d_inner,
                              ngroups=ngroups, nstate=nstate, chunk=min(SSD_CHUNK, ps["t"]),
                              append=append)
                out["ssm"].append(sst)
                return y

            y_all = stacked(mixer, _ssd_rows_per_step(passes[0]["t"], min(SSD_CHUNK, passes[0]["t"])))
            x = _matmul(y_all, p["m_w_out"], layer=l, bm=MM_ROWS_WIDE_K, bn=256, bk=d_inner,
                        epilogue="resid", res=x, gate_tbl=gate, bounds=bounds)
        else:
            j = l - n_a
            q = _matmul(h, p["w_q"], layer=j, out_dtype=MXU_DTYPE, bn=512, out_scale=dh ** -0.5 * LOG2E)
            os_ = []
            for ps, cum in zip(passes, cums):
                if ps["prefill"]:
                    qat, ka, vt = _fox_prep(ps["view"](q), ps["view"](k_all), ps["view"](v_all), cum[0],
                                            t=ps["t"], dh=dh)
                    o = _fox_prefill(qat, ka, vt, dh=dh)
                else:
                    o = _fox_decode(ps["view"](q), ps["k_past"], ps["v_past"], ps["view"](k_all),
                                    ps["view"](v_all), cum[0], cum[1], boff=ps["boff"], t=ps["t"], dh=dh)
                os_.append(o)
            x = _matmul(stack_rows(os_), p["w_o"], layer=j, bm=MM_ROWS_WIDE_K, bn=512,
                        epilogue="resid", res=x, gate_tbl=gate, bounds=bounds)
        shift, scale, gate = tables(2 * l + 1)
        h = _norm_mod(x, scale, shift, bounds)
        up = _matmul(h, p["f_w_up"], layer=l, bn=512)
        ffn_tt = 128

        def ffn_gate(ps, out, append):
            u, tail = _causal_conv(ps["view"](up), ps["ffn0"][l], p["f_conv_w"][l], p["f_conv_b"][l],
                                   boff=ps["boff"], t=ps["t"], col_off=d_ff, cdim=d_ff, tc=d_ff, tt=ffn_tt,
                                   out_dtype=MXU_DTYPE, gate_col_off=0, append=append)
            out["ffn"].append(tail[:, SUBLANES - (p["f_conv_w"].shape[1] - 1):])
            return u

        u_all = stacked(ffn_gate, _tile(passes[0]["t"], ffn_tt, SUBLANES))
        x = _matmul(u_all, p["f_w_down_lp"], layer=l, bm=MM_ROWS_WIDE_K, bn=256, bk=d_ff,
                    epilogue="resid", res=x, gate_tbl=gate, bounds=bounds)
        if l == n_a - 1:
            hkv = _norm_w(x[None], p["kv_norm_w"], boff=0, nb=1, t=r, out_dtype=MXU_DTYPE)[0]
            k_all = _matmul(hkv, p["w_kv"], n=d, bn=512)
            v_all = _matmul(hkv, p["w_kv"], n_off=d, n=d, bn=512)
            w_fg = jnp.pad(p["w_fgate"], ((0, 0), (0, LANES - nh_fox)))
            b_fg = jnp.pad(p["b_fgate"].astype(F32), (0, LANES - nh_fox)).reshape(1, LANES)
            lf_pad = _matmul(hkv, w_fg, bn=LANES, epilogue="logsig", bias=b_fg)
            tb = 256
            for i, ps in enumerate(passes):
                if ps["prefill"]:
                    assert ps["nb"] == 1 and ps["boff"] == 0 and ps["t"] % tb == 0
                    cums[i] = _cumsum_time(lf_pad[None], t=ps["t"], tb=tb)
                else:
                    lo = ps["boff"] * ps["t"]
                    lf_new = lf_pad[lo:lo + ps["nb"] * ps["t"]].reshape(ps["nb"], ps["t"], LANES)
                    lf_p = jnp.pad(ps["lf_past"].astype(F32), ((0, 0), (0, 0), (0, LANES - nh_fox)))
                    lf_all = jnp.concatenate([lf_p, lf_new], axis=1)
                    tot = lf_all.shape[1]
                    lf_all = jnp.pad(lf_all, ((0, 0), (0, (-tot) % tb), (0, 0)))
                    cums[i] = _cumsum_time(lf_all, t=lf_all.shape[1], tb=tb)
    results = []
    for ps, out in zip(passes, outs):
        lo, n = ps["boff"] * ps["t"], ps["nb"] * ps["t"]
        y_out = _norm_w(ps["view"](x), p["final_norm_w"], boff=ps["boff"], nb=ps["nb"], t=ps["t"],
                        out_dtype=x.dtype)
        seq = lambda a, w: a[lo:lo + n, :w].reshape((ps["nb"], ps["t"]) + ((nh_fox, dh) if w == d else (w,)))
        results.append((y_out, jnp.stack(out["ssm"]).astype(x.dtype), jnp.stack(out["conv"]),
                        jnp.stack(out["ffn"]), seq(k_all, d), seq(v_all, d),
                        seq(lf_pad, nh_fox).astype(x.dtype)))
    return results


def kernel(x_prompt, x_sample, c_prompt, c_sample, cache_k, cache_v, cache_logf, state_ssm, state_conv, state_ffn_conv, w_ada, b_ada, m_w_in, m_conv_w, m_conv_b, m_dt_bias, m_a_log, m_d, m_norm_w, m_w_out, kv_norm_w, w_kv, w_fgate, b_fgate, w_q, w_o, f_w_up, f_conv_w, f_conv_b, f_w_down, final_norm_w):
    p = dict(w_ada=w_ada, b_ada=b_ada, m_w_in=m_w_in, m_conv_w=m_conv_w, m_conv_b=m_conv_b,
             m_dt_bias=m_dt_bias, m_a_log=m_a_log, m_d=m_d, m_norm_w=m_norm_w, m_w_out=m_w_out,
             kv_norm_w=kv_norm_w, w_kv=w_kv, w_fgate=w_fgate, b_fgate=b_fgate, w_q=w_q, w_o=w_o,
             f_w_up=f_w_up, f_conv_w=f_conv_w, f_conv_b=f_conv_b, f_w_down=f_w_down,
             final_norm_w=final_norm_w)
    p["f_w_down_lp"] = f_w_down.astype(MXU_DTYPE)
    bp, tp, d = x_prompt.shape
    bs, ts, _ = x_sample.shape
    assert bp == 1 and tp % ts == 0 and ts % PACKED_ROWS == 0
    n_a, depth = m_w_in.shape[0], w_ada.shape[0]
    nheads, hd, nstate = state_ssm.shape[2], state_ssm.shape[3], state_ssm.shape[4]
    past = cache_k.shape[1]
    dtp = x_prompt.dtype
    r = tp + bs * ts

    nseq = bp + bs
    assert nseq <= SEQ_TABLE_ROWS
    c_rows = jnp.pad(jnp.concatenate([c_prompt, c_sample], axis=0), ((0, SEQ_TABLE_ROWS - nseq), (0, 0)))
    mod = _ada_all(c_rows, w_ada, b_ada)

    x = jnp.concatenate([x_prompt.reshape(tp, d), x_sample.reshape(bs * ts, d)], axis=0)
    bounds = ((0, tp, 0),) + tuple((tp + b * ts, tp + (b + 1) * ts, 1 + b) for b in range(bs))
    passes = [
        dict(prefill=True, boff=0, nb=1, t=tp, view=lambda a: a.reshape(1, r, a.shape[-1]),
             ssm0=jnp.zeros((n_a, bp, nheads, hd, nstate), F32),
             conv0=jnp.zeros((n_a, bp, m_conv_w.shape[1] - 1, m_conv_w.shape[2]), dtp),
             ffn0=jnp.zeros((depth, bp, f_conv_w.shape[1] - 1, f_conv_w.shape[2]), dtp)),
        dict(prefill=False, boff=tp // ts, nb=bs, t=ts, view=lambda a: a.reshape(r // ts, ts, a.shape[-1]),
             ssm0=state_ssm, conv0=state_conv, ffn0=state_ffn_conv, lf_past=cache_logf,
             k_past=cache_k, v_past=cache_v),
    ]
    out_p, out_s = _forward(x, mod, passes, bounds, cache_k, cache_v, cache_logf, p)
    return (out_p[0], out_s[0]) + out_p[1:] + out_s[1:]
```
